```python
import math
import jax
import jax.numpy as jnp
from jax import lax
import numpy as np

D_MODEL = 1024
BATCH = 4
SEQ = 8192
DEPTH = 4

HEAD_DIM = 64
N_HEADS = D_MODEL // HEAD_DIM
D_INNER = N_HEADS * HEAD_DIM
N_MIXERS = 4
ROPE_THETA = 500000.0
ROT_DIM = HEAD_DIM // 4
Q_BLOCK = 128
LN_EPS = 1e-5
DN_ALPHA = (2.0 * DEPTH) ** 0.25
DN_BETA = (8.0 * DEPTH) ** -0.25
MOBA_BLOCK = 256
MOBA_TOPK = 3
MOBA_Q_CHUNK = 16
SWA_WINDOW = 128
SWA_KV_HEADS = 4
DILATED_GROUPS = ((128, 1), (512, 4), (2048, 16))

SB_IN = 4 * D_INNER
MOBA_IN = 4 * D_INNER
SWA_IN = 2 * D_INNER + 2 * SWA_KV_HEADS * HEAD_DIM
DIL_IN = (3 * len(DILATED_GROUPS) + 1) * D_INNER

kernel_name = 'hybrid_sb_moba_swa_dilated_deepnorm'


def layer_norm(x, g, b):
    xf = x.astype(jnp.float32)
    mu = jnp.mean(xf, axis=-1, keepdims=True)
    var = jnp.mean(jnp.square(xf - mu), axis=-1, keepdims=True)
    return ((xf - mu) * lax.rsqrt(var + LN_EPS) * g + b).astype(x.dtype)


def split_cols(p, sizes):
    idx = [int(c) for c in np.cumsum(sizes)[:-1]]
    return jnp.split(p, idx, axis=-1)


def rope_tables(seq_len):
    pos = jnp.arange(seq_len, dtype=jnp.float32)
    inv = ROPE_THETA ** (-jnp.arange(0, ROT_DIM, 2, dtype=jnp.float32) / ROT_DIM)
    ang = pos[:, None] * inv[None, :]
    return jnp.cos(ang)[:, None, :], jnp.sin(ang)[:, None, :]


def partial_rope(x, cos, sin):
    half = ROT_DIM // 2
    x1 = x[..., :half].astype(jnp.float32)
    x2 = x[..., half:ROT_DIM].astype(jnp.float32)
    r1 = (x1 * cos - x2 * sin).astype(x.dtype)
    r2 = (x2 * cos + x1 * sin).astype(x.dtype)
    return jnp.concatenate([r1, r2, x[..., ROT_DIM:]], axis=-1)


def stick_breaking_attn(q, k, v):
    B, S, H, Dh = q.shape
    nb = S // Q_BLOCK
    scale = Dh ** -0.5
    qb = q.reshape(B, nb, Q_BLOCK, H, Dh).transpose(1, 0, 3, 2, 4)
    kt = k.transpose(0, 2, 1, 3)
    vt = v.transpose(0, 2, 1, 3)
    kpos = jnp.arange(S)

    def one_block(args):
        qi, bi = args
        tpos = bi * Q_BLOCK + jnp.arange(Q_BLOCK)
        before = kpos[None, :] < tpos[:, None]
        z = jnp.einsum('bhqd,bhsd->bhqs', qi, kt).astype(jnp.float32) * scale
        log_1m = jnp.where(before, jax.nn.log_sigmoid(-z), 0.0)
        between = lax.cumsum(log_1m, axis=3, reverse=True) - log_1m
        log_a = jnp.where(before, jax.nn.log_sigmoid(z) + between, -jnp.inf)
        return jnp.einsum('bhqs,bhsd->bhqd', jnp.exp(log_a).astype(vt.dtype), vt)

    out = lax.map(one_block, (qb, jnp.arange(nb)))
    return out.transpose(1, 0, 3, 2, 4).reshape(B, S, H, Dh)


def moba_attn(q, k, v):
    B, S, H, Dh = q.shape
    scale = Dh ** -0.5
    nkb = -(-S // MOBA_BLOCK)
    pad = nkb * MOBA_BLOCK - S
    qt = q.transpose(0, 2, 1, 3)
    kp = jnp.pad(k.transpose(0, 2, 1, 3), ((0, 0), (0, 0), (0, pad), (0, 0)))
    vp = jnp.pad(v.transpose(0, 2, 1, 3), ((0, 0), (0, 0), (0, pad), (0, 0)))
    kblk = kp.reshape(B, H, nkb, MOBA_BLOCK, Dh)
    vblk = vp.reshape(B, H, nkb, MOBA_BLOCK, Dh)
    k_mean = jnp.mean(kblk.astype(jnp.float32), axis=3)
    gate = jnp.einsum('bhsd,bhnd->bhsn', qt.astype(jnp.float32), k_mean)
    q_blk = jnp.arange(S) // MOBA_BLOCK
    fully_past = jnp.arange(nkb)[None, :] < q_blk[:, None]
    gate = jnp.where(fully_past, gate, -jnp.inf)
    topk = min(MOBA_TOPK, nkb)
    _, sel = lax.top_k(gate, topk)
    sel_ok = sel < q_blk[:, None]

    C = MOBA_Q_CHUNK
    nc = S // C
    n_sel = topk * MOBA_BLOCK
    q_c = qt.reshape(B, H, nc, C, Dh).transpose(2, 0, 1, 3, 4)
    sel_c = sel.reshape(B, H, nc, C, topk).transpose(2, 0, 1, 3, 4)
    ok_c = sel_ok.reshape(B, H, nc, C, topk).transpose(2, 0, 1, 3, 4)
    b_idx = jnp.arange(B)[:, None, None, None]
    h_idx = jnp.arange(H)[None, :, None, None]

    def one_chunk(args):
        qc, selc, okc, ci = args
        t = ci * C + jnp.arange(C)
        k_sel = kblk[b_idx, h_idx, selc]
        v_sel = vblk[b_idx, h_idx, selc]
        own = (ci * C) // MOBA_BLOCK * MOBA_BLOCK
        k_own = lax.dynamic_slice_in_dim(kp, own, MOBA_BLOCK, axis=2)
        v_own = lax.dynamic_slice_in_dim(vp, own, MOBA_BLOCK, axis=2)
        s_sel = jnp.einsum('bhqd,bhqnkd->bhqnk', qc, k_sel).astype(jnp.float32) * scale
        s_sel = jnp.where(okc[..., None], s_sel, -jnp.inf).reshape(B, H, C, n_sel)
        s_own = jnp.einsum('bhqd,bhkd->bhqk', qc, k_own).astype(jnp.float32) * scale
        s_own = jnp.where((own + jnp.arange(MOBA_BLOCK))[None, :] <= t[:, None], s_own, -jnp.inf)
        p = jax.nn.softmax(jnp.concatenate([s_sel, s_own], axis=-1), axis=-1).astype(v.dtype)
        o = jnp.einsum('bhqnk,bhqnkd->bhqd', p[..., :n_sel].reshape(B, H, C, topk, MOBA_BLOCK), v_sel)
        return o + jnp.einsum('bhqk,bhkd->bhqd', p[..., n_sel:], v_own)

    out = lax.map(one_chunk, (q_c, sel_c, ok_c, jnp.arange(nc)))
    return out.transpose(1, 0, 3, 2, 4).reshape(B, S, H, Dh)


def banded_attn(q, k, v, max_back, sink_logits=None):
    N, L, H, Dh = q.shape
    G = k.shape[2]
    R = H // G
    scale = Dh ** -0.5
    nb = -(-L // Q_BLOCK)
    Lp = nb * Q_BLOCK
    n_prev = -(-max_back // Q_BLOCK)
    span = (n_prev + 1) * Q_BLOCK
    qb = jnp.pad(q, ((0, 0), (0, Lp - L), (0, 0), (0, 0))).reshape(N, nb, Q_BLOCK, G, R, Dh)
    kv_pad = ((0, 0), (n_prev * Q_BLOCK, Lp - L), (0, 0), (0, 0))
    kp = jnp.pad(k, kv_pad)
    vp = jnp.pad(v, kv_pad)

    def band(t):
        return jnp.concatenate(
            [t[:, j * Q_BLOCK:j * Q_BLOCK + Lp].reshape(N, nb, Q_BLOCK, G, Dh) for j in range(n_prev + 1)],
            axis=2)

    kb = band(kp)
    vb = band(vp)
    s = jnp.einsum('nbqgrd,nbkgd->nbgrqk', qb, kb).astype(jnp.float32) * scale
    rel = jnp.arange(span) - n_prev * Q_BLOCK
    dist = jnp.arange(Q_BLOCK)[:, None] - rel[None, :]
    key_pos = (jnp.arange(nb) * Q_BLOCK)[:, None] + rel[None, :]
    valid = ((dist >= 0) & (dist <= max_back))[None] & (key_pos >= 0)[:, None, :]
    s = jnp.where(valid[None, :, None, None], s, -jnp.inf)
    m = jnp.max(s, axis=-1, keepdims=True)
    if sink_logits is not None:
        sink = sink_logits.astype(jnp.float32).reshape(1, 1, G, R, 1, 1)
        m = jnp.maximum(m, sink)
    e = jnp.exp(s - m)
    den = jnp.sum(e, axis=-1, keepdims=True)
    if sink_logits is not None:
        den = den + jnp.exp(sink - m)
    o = jnp.einsum('nbgrqk,nbkgd->nbqgrd', (e / den).astype(v.dtype), vb)
    lse = (m + jnp.log(den))[..., 0]
    o = o.reshape(N, Lp, H, Dh)[:, :L]
    lse = lse.transpose(0, 1, 4, 2, 3).reshape(N, Lp, H)[:, :L]
    return o, lse


def dilated_attn(q, k, v, window, dilation):
    B, S, H, Dh = q.shape
    L = S // dilation

    def to_streams(t):
        return t.reshape(B, L, dilation, H, Dh).transpose(0, 2, 1, 3, 4).reshape(B * dilation, L, H, Dh)

    o, lse = banded_attn(to_streams(q), to_streams(k), to_streams(v), window // dilation)
    o = o.reshape(B, dilation, L, H, Dh).transpose(0, 2, 1, 3, 4).reshape(B, S, H, Dh)
    lse = lse.reshape(B, dilation, L, H).transpose(0, 2, 1, 3).reshape(B, S, H)
    return o, lse


def stick_breaking_branch(h, w_in):
    B, S, _ = h.shape
    q, k, v, z = split_cols(h @ w_in, (D_INNER, D_INNER, D_INNER, D_INNER))
    shp = (B, S, N_HEADS, HEAD_DIM)
    y = stick_breaking_attn(q.reshape(shp), k.reshape(shp), v.reshape(shp))
    return y.reshape(B, S, D_INNER), z


def moba_branch(h, w_in, cos, sin):
    B, S, _ = h.shape
    q, k, v, z = split_cols(h @ w_in, (D_INNER, D_INNER, D_INNER, D_INNER))
    shp = (B, S, N_HEADS, HEAD_DIM)
    q = partial_rope(q.reshape(shp), cos, sin)
    k = partial_rope(k.reshape(shp), cos, sin)
    y = moba_attn(q, k, v.reshape(shp))
    return y.reshape(B, S, D_INNER), z


def swa_branch(h, w_in, sinks, cos, sin):
    B, S, _ = h.shape
    kvw = SWA_KV_HEADS * HEAD_DIM
    q, k, v, z = split_cols(h @ w_in, (D_INNER, kvw, kvw, D_INNER))
    q = partial_rope(q.reshape(B, S, N_HEADS, HEAD_DIM), cos, sin)
    k = partial_rope(k.reshape(B, S, SWA_KV_HEADS, HEAD_DIM), cos, sin)
    v = v.reshape(B, S, SWA_KV_HEADS, HEAD_DIM)
    y, _ = banded_attn(q, k, v, SWA_WINDOW - 1, sinks)
    return y.reshape(B, S, D_INNER), z


def dilated_branch(h, w_in, cos, sin):
    B, S, _ = h.shape
    n_g = len(DILATED_GROUPS)
    parts = split_cols(h @ w_in, (D_INNER,) * (3 * n_g + 1))
    z = parts[-1]
    shp = (B, S, N_HEADS, HEAD_DIM)
    outs = []
    lses = []
    for g, (window, dil) in enumerate(DILATED_GROUPS):
        q = partial_rope(parts[3 * g].reshape(shp), cos, sin)
        k = partial_rope(parts[3 * g + 1].reshape(shp), cos, sin)
        v = parts[3 * g + 2].reshape(shp)
        o, lse = dilated_attn(q, k, v, window, dil)
        outs.append(o)
        lses.append(lse)
    wts = jax.nn.softmax(jnp.stack(lses), axis=0)
    o_all = jnp.stack(outs)
    y = jnp.einsum('gbsh,gbshd->bshd', wts.astype(o_all.dtype), o_all)
    return y.reshape(B, S, D_INNER), z


def setup_inputs(seed: int = 0) -> dict:
    key = jax.random.key(seed)
    ks = jax.random.split(key, 20)

    def w(k, fan_in, fan_out, scale=1.0):
        return jax.random.normal(k, (fan_in, fan_out), jnp.float32) * (scale * fan_in ** -0.5)

    def gain(k):
        return 1.0 + 0.02 * jax.random.normal(k, (D_MODEL,), jnp.float32)

    def bias(k):
        return 0.02 * jax.random.normal(k, (D_MODEL,), jnp.float32)

    return {
        'x': jax.random.normal(ks[0], (BATCH, SEQ, D_MODEL), jnp.float32),
        'sb_w_in': w(ks[1], D_MODEL, SB_IN),
        'sb_w_out': w(ks[2], D_INNER, D_MODEL, DN_BETA),
        'ln0_g': gain(ks[3]),
        'ln0_b': bias(ks[4]),
        'moba_w_in': w(ks[5], D_MODEL, MOBA_IN),
        'moba_w_out': w(ks[6], D_INNER, D_MODEL, DN_BETA),
        'ln1_g': gain(ks[7]),
        'ln1_b': bias(ks[8]),
        'swa_w_in': w(ks[9], D_MODEL, SWA_IN),
        'swa_sinks': 0.5 * jax.random.normal(ks[10], (N_HEADS,), jnp.float32),
        'swa_w_out': w(ks[11], D_INNER, D_MODEL, DN_BETA),
        'ln2_g': gain(ks[12]),
        'ln2_b': bias(ks[13]),
        'dil_w_in': w(ks[14], D_MODEL, DIL_IN),
        'dil_w_out': w(ks[15], D_INNER, D_MODEL, DN_BETA),
        'ln3_g': gain(ks[16]),
        'ln3_b': bias(ks[17]),
    }


def reference(x, sb_w_in, sb_w_out, ln0_g, ln0_b, moba_w_in, moba_w_out, ln1_g, ln1_b,
              swa_w_in, swa_sinks, swa_w_out, ln2_g, ln2_b, dil_w_in, dil_w_out, ln3_g, ln3_b):
    S = x.shape[1]
    cos, sin = rope_tables(S)
    mixers = (
        lambda h: stick_breaking_branch(h, sb_w_in),
        lambda h: moba_branch(h, moba_w_in, cos, sin),
        lambda h: swa_branch(h, swa_w_in, swa_sinks, cos, sin),
        lambda h: dilated_branch(h, dil_w_in, cos, sin),
    )
    w_outs = (sb_w_out, moba_w_out, swa_w_out, dil_w_out)
    ln_gs = (ln0_g, ln1_g, ln2_g, ln3_g)
    ln_bs = (ln0_b, ln1_b, ln2_b, ln3_b)
    for i in range(DEPTH):
        m = i % N_MIXERS
        y, z = mixers[m](x)
        out = (y * jax.nn.silu(z)) @ w_outs[m]
        x = layer_norm(DN_ALPHA * x + out, ln_gs[m], ln_bs[m])
    return x
```

```python
import functools

import jax
import jax.numpy as jnp
import numpy as np
from jax import lax
from jax.experimental import pallas as pl
from jax.experimental.pallas import tpu as pltpu

D_MODEL = 1024
HEAD_DIM = 64
N_HEADS = D_MODEL // HEAD_DIM
D_INNER = N_HEADS * HEAD_DIM
ROPE_THETA = 500000.0
ROT_DIM = HEAD_DIM // 4
LN_EPS = 1e-5
DEPTH = 4
DN_ALPHA = (2.0 * DEPTH) ** 0.25
MOBA_BLOCK = 256
MOBA_TOPK = 3
SWA_WINDOW = 128
SWA_KV_HEADS = 4
DILATED_GROUPS = ((128, 1), (512, 4), (2048, 16))
BAND_BLOCK = 128
Q_SCALE = HEAD_DIM ** -0.5

LANES = 128
PAIRS = D_INNER // LANES
VMEM_LIMIT = 56 * 1024 * 1024

F32 = jnp.float32
BF16 = jnp.bfloat16
NEG_INF = float("-inf")


def _cparams(n_axes):
    return pltpu.CompilerParams(dimension_semantics=("arbitrary",) * n_axes, vmem_limit_bytes=VMEM_LIMIT)


def _nt_dot(a, b):
    return lax.dot_general(a, b, (((1,), (1,)), ((), ())), preferred_element_type=F32)


def _rope_lane_tables(seq_len):
    half = ROT_DIM // 2
    pos = jnp.arange(seq_len, dtype=F32)
    inv = ROPE_THETA ** (-jnp.arange(0, ROT_DIM, 2, dtype=F32) / ROT_DIM)
    ang = pos[:, None] * inv[None, :]
    cos, sin = jnp.cos(ang), jnp.sin(ang)
    hl = np.arange(LANES) % HEAD_DIM
    idx = jnp.asarray(hl % half)
    cos_l, sin_l = cos[:, idx], sin[:, idx]
    c = jnp.where(jnp.asarray(hl < ROT_DIM)[None, :], cos_l, 1.0)
    s1 = jnp.where(jnp.asarray(hl < half)[None, :], -sin_l, 0.0)
    s2 = jnp.where(jnp.asarray((hl >= half) & (hl < ROT_DIM))[None, :], sin_l, 0.0)
    return c.astype(F32), s1.astype(F32), s2.astype(F32)


def _apply_rope(y, c, s1, s2):
    parts = []
    for t in range(y.shape[1] // LANES):
        yt = y[:, t * LANES:(t + 1) * LANES]
        up = pltpu.roll(yt, LANES - ROT_DIM // 2, 1)
        dn = pltpu.roll(yt, ROT_DIM // 2, 1)
        parts.append(yt * c + up * s1 + dn * s2)
    return parts[0] if len(parts) == 1 else jnp.concatenate(parts, axis=1)


def _proj_kernel(*refs, segs, use_rope, kmean_seg):
    x_ref, w_ref = refs[0], refs[1]
    pos = 2
    if use_rope:
        c, s1, s2 = refs[2][...], refs[3][...], refs[4][...]
        pos = 5
    outs = refs[pos:]
    xb = x_ref[...].astype(BF16)
    for si, (col0, width, rope, scale, oi) in enumerate(segs):
        y = jnp.dot(xb, w_ref[:, col0:col0 + width], preferred_element_type=F32)
        if rope:
            y = _apply_rope(y, c, s1, s2)
        if scale != 1.0:
            y = y * scale
        outs[oi][...] = y.astype(outs[oi].dtype)
        if kmean_seg == si:
            tm = y.shape[0]
            km = outs[-1]
            for blk in range(tm // MOBA_BLOCK):
                rows = y[blk * MOBA_BLOCK:(blk + 1) * MOBA_BLOCK, :]
                km[0, blk:blk + 1, :] = jnp.sum(rows, axis=0, keepdims=True) * (1.0 / MOBA_BLOCK)


def _project(x2, w_bf, segs, out_widths, out_dtypes, rope_tabs=None, kmean_seg=None, tm=256):
    T, D = x2.shape
    N = w_bf.shape[1]
    seq = rope_tabs[0].shape[0] if rope_tabs is not None else None
    in_specs = [pl.BlockSpec((tm, D), lambda i: (i, 0)),
                pl.BlockSpec((D, N), lambda i: (0, 0), pipeline_mode=pl.Buffered(1))]
    args = [x2, w_bf]
    if rope_tabs is not None:
        nblk = seq // tm
        for t in rope_tabs:
            in_specs.append(pl.BlockSpec((tm, LANES), lambda i: (i % nblk, 0)))
            args.append(t)
    out_shape = [jax.ShapeDtypeStruct((T, w), dt) for w, dt in zip(out_widths, out_dtypes)]
    out_specs = [pl.BlockSpec((tm, w), lambda i: (i, 0)) for w in out_widths]
    if kmean_seg is not None:
        nb = tm // MOBA_BLOCK
        out_shape.append(jax.ShapeDtypeStruct((T // tm, nb, D_INNER), F32))
        out_specs.append(pl.BlockSpec((1, nb, D_INNER), lambda i: (i, 0, 0)))
    kern = functools.partial(_proj_kernel, segs=segs, use_rope=rope_tabs is not None, kmean_seg=kmean_seg)
    return pl.pallas_call(
        kern, grid=(T // tm,), in_specs=in_specs, out_specs=out_specs, out_shape=out_shape,
        compiler_params=_cparams(1), name="in_proj")(*args)


def _sb_kernel(q_ref, k_ref, v_ref, o_ref, *, tq):
    i = pl.program_id(2)
    q = q_ref[0]
    lane = lax.broadcasted_iota(jnp.int32, (tq, LANES), 1)
    row = lax.broadcasted_iota(jnp.int32, (tq, tq), 0)
    col = lax.broadcasted_iota(jnp.int32, (tq, tq), 1)
    before = col < row
    suffix = jnp.where(row > col, 1.0, 0.0).astype(BF16)

    def tile(qm, j, c, acc, diag):
        start = pl.multiple_of(j * tq, tq)
        kb = k_ref[0, pl.ds(start, tq), :]
        vb = v_ref[0, pl.ds(start, tq), :]
        z = _nt_dot(qm, kb)
        sp = jnp.maximum(z, 0.0) + jnp.log(1.0 + jnp.exp(-jnp.abs(z)))
        l1m = -sp
        if diag:
            l1m = jnp.where(before, l1m, 0.0)
        between = jnp.dot(l1m.astype(BF16), suffix, preferred_element_type=F32) + c
        a = jnp.exp(z - sp + between)
        if diag:
            a = jnp.where(before, a, 0.0)
        acc = acc + jnp.dot(a.astype(BF16), vb, preferred_element_type=F32)
        c = c + jnp.sum(l1m, axis=1, keepdims=True)
        return c, acc

    accs = []
    for h in range(2):
        qm = jnp.where((lane >= h * HEAD_DIM) & (lane < (h + 1) * HEAD_DIM), q, jnp.zeros_like(q))
        c0 = jnp.zeros((tq, 1), F32)
        a0 = jnp.zeros((tq, LANES), F32)
        c1, a1 = tile(qm, i, c0, a0, True)

        def body(n, carry, qm=qm):
            return tile(qm, i - 1 - n, carry[0], carry[1], False)

        _, acc = lax.fori_loop(0, i, body, (c1, a1))
        accs.append(acc)
    o_ref[0] = jnp.where(lane < HEAD_DIM, accs[0], accs[1])


def _sb_attention(q, k, v, tq=256):
    B, S, _ = q.shape
    grid = (B, PAIRS, S // tq)
    return pl.pallas_call(
        functools.partial(_sb_kernel, tq=tq), grid=grid,
        in_specs=[pl.BlockSpec((1, tq, LANES), lambda b, p, i: (b, i, p)),
                  pl.BlockSpec((1, S, LANES), lambda b, p, i: (b, 0, p)),
                  pl.BlockSpec((1, S, LANES), lambda b, p, i: (b, 0, p))],
        out_specs=pl.BlockSpec((1, tq, LANES), lambda b, p, i: (b, i, p)),
        out_shape=jax.ShapeDtypeStruct((B, S, D_INNER), F32),
        compiler_params=_cparams(3), name="sb_attn")(q, k, v)


def _moba_kernel(q_ref, k_ref, v_ref, km_ref, o_ref, *, nkb):
    tq = MOBA_BLOCK
    i = pl.program_id(2)
    q = q_ref[0]
    km = km_ref[0]
    lane = lax.broadcasted_iota(jnp.int32, (tq, LANES), 1)
    row = lax.broadcasted_iota(jnp.int32, (tq, tq), 0)
    col = lax.broadcasted_iota(jnp.int32, (tq, tq), 1)
    causal = col <= row
    blk = lax.broadcasted_iota(jnp.int32, (tq, nkb), 1)
    blk_f = blk.astype(F32)
    past = blk < i

    accs = []
    for h in range(2):
        in_head = (lane >= h * HEAD_DIM) & (lane < (h + 1) * HEAD_DIM)
        qm = jnp.where(in_head, q, jnp.zeros_like(q))
        gate = lax.dot_general(qm.astype(F32), km, (((1,), (1,)), ((), ())),
                               precision=lax.Precision.HIGHEST, preferred_element_type=F32)
        g = jnp.where(past, gate, NEG_INF)
        sel = jnp.zeros((tq, nkb), F32)
        for _ in range(MOBA_TOPK):
            mx = jnp.max(g, axis=1, keepdims=True)
            first = jnp.min(jnp.where(g == mx, blk_f, float(nkb)), axis=1, keepdims=True)
            pick = blk_f == first
            sel = jnp.where(pick & past, 1.0, sel)
            g = jnp.where(pick, NEG_INF, g)

        start = pl.multiple_of(i * tq, tq)
        s = jnp.where(causal, _nt_dot(qm, k_ref[0, pl.ds(start, tq), :]), NEG_INF)
        m0 = jnp.max(s, axis=1, keepdims=True)
        p = jnp.exp(s - m0)
        l0 = jnp.sum(p, axis=1, keepdims=True)
        a0 = jnp.dot(p.astype(BF16), v_ref[0, pl.ds(start, tq), :], preferred_element_type=F32)

        def body(j, carry, qm=qm, sel=sel):
            m, l, acc = carry
            st = pl.multiple_of(j * tq, tq)
            chosen = jnp.max(jnp.where(blk == j, sel, 0.0), axis=1, keepdims=True) > 0.5
            s = jnp.where(chosen, _nt_dot(qm, k_ref[0, pl.ds(st, tq), :]), NEG_INF)
            m_new = jnp.maximum(m, jnp.max(s, axis=1, keepdims=True))
            alpha = jnp.exp(m - m_new)
            p = jnp.exp(s - m_new)
            l = alpha * l + jnp.sum(p, axis=1, keepdims=True)
            acc = alpha * acc + jnp.dot(p.astype(BF16), v_ref[0, pl.ds(st, tq), :], preferred_element_type=F32)
            return m_new, l, acc

        _, l, acc = lax.fori_loop(0, i, body, (m0, l0, a0))
        accs.append(acc / l)
    o_ref[0] = jnp.where(lane < HEAD_DIM, accs[0], accs[1])


def _moba_attention(q, k, v, kmean):
    B, S, _ = q.shape
    nkb = S // MOBA_BLOCK
    grid = (B, PAIRS, nkb)
    return pl.pallas_call(
        functools.partial(_moba_kernel, nkb=nkb), grid=grid,
        in_specs=[pl.BlockSpec((1, MOBA_BLOCK, LANES), lambda b, p, i: (b, i, p)),
                  pl.BlockSpec((1, S, LANES), lambda b, p, i: (b, 0, p)),
                  pl.BlockSpec((1, S, LANES), lambda b, p, i: (b, 0, p)),
                  pl.BlockSpec((1, nkb, LANES), lambda b, p, i: (b, 0, p))],
        out_specs=pl.BlockSpec((1, MOBA_BLOCK, LANES), lambda b, p, i: (b, i, p)),
        out_shape=jax.ShapeDtypeStruct((B, S, D_INNER), F32),
        compiler_params=_cparams(3), name="moba_attn")(q, k, v, kmean)


def _band_kernel(*refs, max_back, kv_heads, use_sinks, want_lse):
    tq = BAND_BLOCK
    pos = 0
    if use_sinks:
        sink_ref = refs[0]
        pos = 1
    q_ref, kp_ref, kc_ref, vp_ref, vc_ref = refs[pos:pos + 5]
    o_ref = refs[pos + 5]
    lse_ref = refs[pos + 6] if want_lse else None
    i = pl.program_id(2)
    rep = N_HEADS // kv_heads

    lane = lax.broadcasted_iota(jnp.int32, (tq, LANES), 1)
    row = lax.broadcasted_iota(jnp.int32, (tq, 2 * tq), 0)
    col = lax.broadcasted_iota(jnp.int32, (tq, 2 * tq), 1)
    dist = row - col + tq
    first_key = jnp.where(i > 0, 0, tq)
    valid = (dist >= 0) & (dist <= max_back) & (col >= first_key)
    low_half = lane < HEAD_DIM

    kcat = jnp.concatenate([kp_ref[0], kc_ref[0]], axis=0)
    vcat = jnp.concatenate([vp_ref[0], vc_ref[0]], axis=0)
    lse_acc = jnp.zeros((tq, LANES), F32)
    for pr in range(PAIRS):
        qf = q_ref[0, :, pr * LANES:(pr + 1) * LANES].astype(F32)
        q_same = qf.astype(BF16)
        q_swap = pltpu.roll(qf, HEAD_DIM, 1).astype(BF16)
        outs = []
        for hh in range(2):
            h = 2 * pr + hh
            g = h // rep
            gh = g % 2
            qsrc = q_same if gh == hh else q_swap
            qm = jnp.where(low_half == (gh == 0), qsrc, jnp.zeros_like(qsrc))
            kt = kcat[:, (g // 2) * LANES:(g // 2 + 1) * LANES]
            vt = vcat[:, (g // 2) * LANES:(g // 2 + 1) * LANES]
            s = jnp.where(valid, _nt_dot(qm, kt), NEG_INF)
            m = jnp.max(s, axis=1, keepdims=True)
            if use_sinks:
                sink = sink_ref[h]
                m = jnp.maximum(m, sink)
            e = jnp.exp(s - m)
            den = jnp.sum(e, axis=1, keepdims=True)
            if use_sinks:
                den = den + jnp.exp(sink - m)
            o = jnp.dot(e.astype(BF16), vt, preferred_element_type=F32) / den
            if gh != hh:
                o = pltpu.roll(o, HEAD_DIM, 1)
            outs.append(o)
            if want_lse:
                lse_acc = jnp.where(lane == h, m + jnp.log(den), lse_acc)
        o_ref[0, :, pr * LANES:(pr + 1) * LANES] = jnp.where(low_half, outs[0], outs[1])
    if want_lse:
        lse_ref[0] = lse_acc


def _band_attention(q, k, v, *, dilation, max_back, kv_heads, sinks=None, want_lse=False):
    B, S, _ = q.shape
    L = S // dilation
    kvw = kv_heads * HEAD_DIM
    tq = BAND_BLOCK
    qv = q.reshape(B, L, dilation * D_INNER)
    kv_ = k.reshape(B, L, dilation * kvw)
    vv = v.reshape(B, L, dilation * kvw)
    grid = (B, dilation, L // tq)
    cur = lambda b, r, i: (b, i, r)
    prev = lambda b, r, i: (b, jnp.maximum(i - 1, 0), r)
    in_specs = [pl.BlockSpec((1, tq, D_INNER), cur),
                pl.BlockSpec((1, tq, kvw), prev), pl.BlockSpec((1, tq, kvw), cur),
                pl.BlockSpec((1, tq, kvw), prev), pl.BlockSpec((1, tq, kvw), cur)]
    args = [qv, kv_, kv_, vv, vv]
    if sinks is not None:
        in_specs = [pl.BlockSpec(memory_space=pltpu.SMEM)] + in_specs
        args = [sinks] + args
    out_shape = [jax.ShapeDtypeStruct((B, L, dilation * D_INNER), F32)]
    out_specs = [pl.BlockSpec((1, tq, D_INNER), cur)]
    if want_lse:
        out_shape.append(jax.ShapeDtypeStruct((B, L, dilation * LANES), F32))
        out_specs.append(pl.BlockSpec((1, tq, LANES), cur))
    kern = functools.partial(_band_kernel, max_back=max_back, kv_heads=kv_heads,
                             use_sinks=sinks is not None, want_lse=want_lse)
    res = pl.pallas_call(kern, grid=grid, in_specs=in_specs, out_specs=out_specs, out_shape=out_shape,
                         compiler_params=_cparams(3), name="band_attn")(*args)
    o = res[0].reshape(B, S, D_INNER)
    if want_lse:
        return o, res[1].reshape(B, S, LANES)
    return o


def _out_kernel(*refs, n_groups):
    ys = refs[:n_groups]
    pos = n_groups
    if n_groups > 1:
        lses = refs[pos:pos + n_groups]
        expand_ref = refs[pos + n_groups]
        pos += n_groups + 1
    z_ref, x_ref, w_ref, g_ref, b_ref, o_ref = refs[pos:pos + 6]
    if n_groups == 1:
        y = ys[0][...]
    else:
        ls = [r[...] for r in lses]
        mx = functools.reduce(jnp.maximum, ls)
        es = [jnp.exp(l - mx) for l in ls]
        tot = functools.reduce(lambda a, b: a + b, es)
        y = None
        for e, yr in zip(es, ys):
            wts = jnp.dot(e / tot, expand_ref[...], precision=lax.Precision.HIGHEST,
                          preferred_element_type=F32)
            y = wts * yr[...] if y is None else y + wts * yr[...]
    z = z_ref[...]
    u = (y * (z * jax.nn.sigmoid(z))).astype(BF16)
    r = DN_ALPHA * x_ref[...] + jnp.dot(u, w_ref[...], preferred_element_type=F32)
    mu = jnp.mean(r, axis=1, keepdims=True)
    d = r - mu
    var = jnp.mean(d * d, axis=1, keepdims=True)
    o_ref[...] = d * lax.rsqrt(var + LN_EPS) * g_ref[...] + b_ref[...]


def _out_block(ys, lses, z, x2, w_out_bf, g, b, tm=256):
    T, D = x2.shape
    n_groups = len(ys)
    row = pl.BlockSpec((tm, D), lambda i: (i, 0))
    in_specs = [row] * n_groups
    args = list(ys)
    if n_groups > 1:
        in_specs += [pl.BlockSpec((tm, LANES), lambda i: (i, 0))] * n_groups
        args += list(lses)
        expand = (np.arange(LANES)[:, None] == (np.arange(D_INNER) // HEAD_DIM)[None, :]).astype(np.float32)
        in_specs.append(pl.BlockSpec((LANES, D_INNER), lambda i: (0, 0)))
        args.append(jnp.asarray(expand))
    in_specs += [row, row, pl.BlockSpec((D_INNER, D), lambda i: (0, 0)),
                 pl.BlockSpec((1, D), lambda i: (0, 0)), pl.BlockSpec((1, D), lambda i: (0, 0))]
    args += [z, x2, w_out_bf, g.reshape(1, D), b.reshape(1, D)]
    return pl.pallas_call(
        functools.partial(_out_kernel, n_groups=n_groups), grid=(T // tm,),
        in_specs=in_specs, out_specs=row, out_shape=jax.ShapeDtypeStruct((T, D), F32),
        compiler_params=_cparams(1), name="out_ln")(*args)


def kernel(x, sb_w_in, sb_w_out, ln0_g, ln0_b, moba_w_in, moba_w_out, ln1_g, ln1_b,
           swa_w_in, swa_sinks, swa_w_out, ln2_g, ln2_b, dil_w_in, dil_w_out, ln3_g, ln3_b):
    B, S, D = x.shape
    T = B * S
    x2 = x.reshape(T, D)
    rope = _rope_lane_tables(S)
    W = D_INNER

    segs = ((0, W, False, Q_SCALE, 0), (W, W, False, 1.0, 1), (2 * W, W, False, 1.0, 2), (3 * W, W, False, 1.0, 3))
    q, k, v, z = _project(x2, sb_w_in.astype(BF16), segs, (W, W, W, W), (BF16, BF16, BF16, F32))
    y = _sb_attention(q.reshape(B, S, W), k.reshape(B, S, W), v.reshape(B, S, W))
    x2 = _out_block([y.reshape(T, W)], None, z, x2, sb_w_out.astype(BF16), ln0_g, ln0_b)

    segs = ((0, W, True, Q_SCALE, 0), (W, W, True, 1.0, 1), (2 * W, W, False, 1.0, 2), (3 * W, W, False, 1.0, 3))
    q, k, v, z, kmean = _project(x2, moba_w_in.astype(BF16), segs, (W, W, W, W), (BF16, BF16, BF16, F32),
                                 rope_tabs=rope, kmean_seg=1)
    y = _moba_attention(q.reshape(B, S, W), k.reshape(B, S, W), v.reshape(B, S, W),
                        kmean.reshape(B, S // MOBA_BLOCK, W))
    x2 = _out_block([y.reshape(T, W)], None, z, x2, moba_w_out.astype(BF16), ln1_g, ln1_b)

    kvw = SWA_KV_HEADS * HEAD_DIM
    segs = ((0, W, True, Q_SCALE, 0), (W, kvw, True, 1.0, 1), (W + kvw, kvw, False, 1.0, 2),
            (W + 2 * kvw, W, False, 1.0, 3))
    q, k, v, z = _project(x2, swa_w_in.astype(BF16), segs, (W, kvw, kvw, W), (BF16, BF16, BF16, F32),
                          rope_tabs=rope)
    y = _band_attention(q.reshape(B, S, W), k.reshape(B, S, kvw), v.reshape(B, S, kvw), dilation=1,
                        max_back=SWA_WINDOW - 1, kv_heads=SWA_KV_HEADS, sinks=swa_sinks.astype(F32))
    x2 = _out_block([y.reshape(T, W)], None, z, x2, swa_w_out.astype(BF16), ln2_g, ln2_b)

    n_g = len(DILATED_GROUPS)
    segs = []
    for g in range(n_g):
        segs += [((3 * g) * W, W, True, Q_SCALE, 3 * g), ((3 * g + 1) * W, W, True, 1.0, 3 * g + 1),
                 ((3 * g + 2) * W, W, False, 1.0, 3 * g + 2)]
    segs.append((3 * n_g * W, W, False, 1.0, 3 * n_g))
    outs = _project(x2, dil_w_in.astype(BF16), tuple(segs), (W,) * (3 * n_g + 1), (BF16,) * (3 * n_g) + (F32,),
                    rope_tabs=rope)
    z = outs[-1]
    ys, lses = [], []
    for g, (window, dil) in enumerate(DILATED_GROUPS):
        qg, kg, vg = (outs[3 * g + t].reshape(B, S, W) for t in range(3))
        o, lse = _band_attention(qg, kg, vg, dilation=dil, max_back=window // dil, kv_heads=N_HEADS,
                                 want_lse=True)
        ys.append(o.reshape(T, W))
        lses.append(lse.reshape(T, LANES))
    x2 = _out_block(ys, lses, z, x2, dil_w_out.astype(BF16), ln3_g, ln3_b)
    return x2.reshape(B, S, D)
```

```python
import functools

import jax
import jax.numpy as jnp
import numpy as np
from jax import lax
from jax.experimental import pallas as pl
from jax.experimental.pallas import tpu as pltpu

D_MODEL = 1024
HEAD_DIM = 64
N_HEADS = D_MODEL // HEAD_DIM
D_INNER = N_HEADS * HEAD_DIM
ROPE_THETA = 500000.0
ROT_DIM = HEAD_DIM // 4
LN_EPS = 1e-5
DEPTH = 4
DN_ALPHA = (2.0 * DEPTH) ** 0.25
MOBA_BLOCK = 256
MOBA_TOPK = 3
SWA_WINDOW = 128
SWA_KV_HEADS = 4
DILATED_GROUPS = ((128, 1), (512, 4), (2048, 16))
BAND_BLOCK = 128
ATTN_TILE = 512
SB_SUB = 256
Q_SCALE = HEAD_DIM ** -0.5
Q_SCALE_LOG2 = Q_SCALE * float(np.log2(np.e))

LANES = 128
PAIRS = D_INNER // LANES
VMEM_LIMIT = 56 * 1024 * 1024

F32 = jnp.float32
BF16 = jnp.bfloat16
NEG_INF = float("-inf")
MASKED = -2.0 ** 60


def _cparams(n_axes):
    return pltpu.CompilerParams(dimension_semantics=("arbitrary",) * n_axes, vmem_limit_bytes=VMEM_LIMIT)


def _nt_dot(a, b):
    return lax.dot_general(a, b, (((1,), (1,)), ((), ())), preferred_element_type=F32)


def _rope_lane_tables(seq_len):
    half = ROT_DIM // 2
    pos = jnp.arange(seq_len, dtype=F32)
    inv = ROPE_THETA ** (-jnp.arange(0, ROT_DIM, 2, dtype=F32) / ROT_DIM)
    ang = pos[:, None] * inv[None, :]
    cos, sin = jnp.cos(ang), jnp.sin(ang)
    hl = np.arange(LANES) % HEAD_DIM
    idx = jnp.asarray(hl % half)
    cos_l, sin_l = cos[:, idx], sin[:, idx]
    c = jnp.where(jnp.asarray(hl < ROT_DIM)[None, :], cos_l, 1.0)
    s1 = jnp.where(jnp.asarray(hl < half)[None, :], -sin_l, 0.0)
    s2 = jnp.where(jnp.asarray((hl >= half) & (hl < ROT_DIM))[None, :], sin_l, 0.0)
    return c.astype(F32), s1.astype(F32), s2.astype(F32)


def _apply_rope(y, c, s1, s2):
    parts = []
    for t in range(y.shape[1] // LANES):
        yt = y[:, t * LANES:(t + 1) * LANES]
        up = pltpu.roll(yt, LANES - ROT_DIM // 2, 1)
        dn = pltpu.roll(yt, ROT_DIM // 2, 1)
        parts.append(yt * c + up * s1 + dn * s2)
    return parts[0] if len(parts) == 1 else jnp.concatenate(parts, axis=1)


def _proj_kernel(*refs, segs, use_rope, kmean_seg):
    x_ref, w_ref = refs[0], refs[1]
    pos = 2
    if use_rope:
        c, s1, s2 = refs[2][...], refs[3][...], refs[4][...]
        pos = 5
    outs = refs[pos:]
    xb = x_ref[...].astype(BF16)
    for si, (col0, width, rope, scale, oi) in enumerate(segs):
        y = jnp.dot(xb, w_ref[:, col0:col0 + width], preferred_element_type=F32)
        if rope:
            y = _apply_rope(y, c, s1, s2)
        if scale != 1.0:
            y = y * scale
        outs[oi][...] = y.astype(outs[oi].dtype)
        if kmean_seg == si:
            tm = y.shape[0]
            km = outs[-1]
            for blk in range(tm // MOBA_BLOCK):
                rows = y[blk * MOBA_BLOCK:(blk + 1) * MOBA_BLOCK, :]
                km[0, blk:blk + 1, :] = jnp.sum(rows, axis=0, keepdims=True) * (1.0 / MOBA_BLOCK)


def _project(x2, w_bf, segs, out_widths, out_dtypes, rope_tabs=None, kmean_seg=None, tm=256):
    T, D = x2.shape
    N = w_bf.shape[1]
    seq = rope_tabs[0].shape[0] if rope_tabs is not None else None
    in_specs = [pl.BlockSpec((tm, D), lambda i: (i, 0)),
                pl.BlockSpec((D, N), lambda i: (0, 0), pipeline_mode=pl.Buffered(1))]
    args = [x2, w_bf]
    if rope_tabs is not None:
        nblk = seq // tm
        for t in rope_tabs:
            in_specs.append(pl.BlockSpec((tm, LANES), lambda i: (i % nblk, 0)))
            args.append(t)
    out_shape = [jax.ShapeDtypeStruct((T, w), dt) for w, dt in zip(out_widths, out_dtypes)]
    out_specs = [pl.BlockSpec((tm, w), lambda i: (i, 0)) for w in out_widths]
    if kmean_seg is not None:
        nb = tm // MOBA_BLOCK
        out_shape.append(jax.ShapeDtypeStruct((T // tm, nb, D_INNER), F32))
        out_specs.append(pl.BlockSpec((1, nb, D_INNER), lambda i: (i, 0, 0)))
    kern = functools.partial(_proj_kernel, segs=segs, use_rope=rope_tabs is not None, kmean_seg=kmean_seg)
    return pl.pallas_call(
        kern, grid=(T // tm,), in_specs=in_specs, out_specs=out_specs, out_shape=out_shape,
        compiler_params=_cparams(1), name="in_proj")(*args)


def _sb_kernel(q_ref, k_ref, v_ref, o_ref, *, tq):
    i = pl.program_id(2)
    q = q_ref[0]
    lane = lax.broadcasted_iota(jnp.int32, (tq, LANES), 1)
    zero = jnp.zeros_like(q)
    q2 = jnp.concatenate([jnp.where(lane < HEAD_DIM, q, zero), jnp.where(lane >= HEAD_DIM, q, zero)], axis=0)
    sub = SB_SUB
    row = lax.broadcasted_iota(jnp.int32, (sub, sub), 0)
    col = lax.broadcasted_iota(jnp.int32, (sub, sub), 1)
    suffix = jnp.where(row > col, 1.0, 0.0).astype(BF16)

    def tile(j, c, acc, diag):
        start = pl.multiple_of(j * tq, tq)
        kb = k_ref[0, pl.ds(start, tq), :]
        vb = v_ref[0, pl.ds(start, tq), :]
        z = _nt_dot(q2, kb)
        neg_abs = lax.bitcast_convert_type(lax.bitcast_convert_type(z, jnp.int32) | jnp.int32(-2 ** 31), F32)
        log_beta = jnp.minimum(z, 0.0) - jnp.log2(1.0 + jnp.exp2(neg_abs))
        l1m = log_beta - z
        if diag:
            qrow = lax.broadcasted_iota(jnp.int32, (2 * tq, tq), 0) & (tq - 1)
            before = lax.broadcasted_iota(jnp.int32, (2 * tq, tq), 1) < qrow
            l1m = jnp.where(before, l1m, 0.0)
            log_beta = jnp.where(before, log_beta, NEG_INF)
        lb = l1m.astype(BF16)
        parts = []
        for s in reversed(range(tq // sub)):
            blk = slice(s * sub, (s + 1) * sub)
            parts.append(jnp.dot(lb[:, blk], suffix, preferred_element_type=F32) + c)
            c = c + jnp.sum(l1m[:, blk], axis=1, keepdims=True)
        between = jnp.concatenate(parts[::-1], axis=1)
        a = jnp.exp2(log_beta + between)
        acc = acc + jnp.dot(a.astype(BF16), vb, preferred_element_type=F32)
        return c, acc

    c, acc = tile(i, jnp.zeros((2 * tq, 1), F32), jnp.zeros((2 * tq, LANES), F32), True)
    _, acc = lax.fori_loop(0, i, lambda n, cr: tile(i - 1 - n, cr[0], cr[1], False), (c, acc))
    o_ref[0] = jnp.where(lane < HEAD_DIM, acc[:tq], acc[tq:])


def _sb_attention(q, k, v, tq=ATTN_TILE):
    B, S, _ = q.shape
    grid = (B, PAIRS, S // tq)
    return pl.pallas_call(
        functools.partial(_sb_kernel, tq=tq), grid=grid,
        in_specs=[pl.BlockSpec((1, tq, LANES), lambda b, p, i: (b, i, p)),
                  pl.BlockSpec((1, S, LANES), lambda b, p, i: (b, 0, p)),
                  pl.BlockSpec((1, S, LANES), lambda b, p, i: (b, 0, p))],
        out_specs=pl.BlockSpec((1, tq, LANES), lambda b, p, i: (b, i, p)),
        out_shape=jax.ShapeDtypeStruct((B, S, D_INNER), F32),
        compiler_params=_cparams(3), name="sb_attn")(q, k, v)


def _moba_kernel(q_ref, k_ref, v_ref, km_ref, o_ref, *, nkb, tq):
    i = pl.program_id(2)
    q = q_ref[0]
    km = km_ref[0]
    lane = lax.broadcasted_iota(jnp.int32, (tq, LANES), 1)
    per_tile = tq // MOBA_BLOCK
    blk = lax.broadcasted_iota(jnp.int32, (2 * tq, nkb), 1)
    blk_f = blk.astype(F32)
    qrow = lax.broadcasted_iota(jnp.int32, (2 * tq, nkb), 0) & (tq - 1)
    q_blk = i * per_tile + qrow // MOBA_BLOCK
    past = blk < q_blk

    zero = jnp.zeros_like(q)
    q2 = jnp.concatenate([jnp.where(lane < HEAD_DIM, q, zero), jnp.where(lane >= HEAD_DIM, q, zero)], axis=0)
    km_hi = km.astype(BF16)
    km_lo = (km - km_hi.astype(F32)).astype(BF16)
    gate = _nt_dot(jnp.concatenate([q2, q2], axis=1), jnp.concatenate([km_hi, km_lo], axis=1))
    g = jnp.where(past, gate, NEG_INF)
    sel = jnp.zeros((2 * tq, nkb), jnp.bool_)
    for _ in range(MOBA_TOPK):
        mx = jnp.max(g, axis=1, keepdims=True)
        first = jnp.min(jnp.where(g == mx, blk_f, float(nkb)), axis=1, keepdims=True)
        pick = blk_f == first
        sel = sel | (pick & past)
        g = jnp.where(pick, NEG_INF, g)
    bias = jnp.where(sel | (blk == q_blk), 0.0, MASKED).astype(BF16)
    pad = jnp.zeros((2 * tq, LANES - nkb), BF16)
    qx = jnp.concatenate([q2, bias, pad], axis=1)
    lane_k = lax.broadcasted_iota(jnp.int32, (tq, LANES), 1)
    key_blk = lax.broadcasted_iota(jnp.int32, (tq, LANES), 0) // MOBA_BLOCK

    def scores(j):
        st = pl.multiple_of(j * tq, tq)
        onehot = jnp.where(lane_k == j * per_tile + key_blk, 1.0, 0.0).astype(BF16)
        kx = jnp.concatenate([k_ref[0, pl.ds(st, tq), :], onehot], axis=1)
        return _nt_dot(qx, kx), v_ref[0, pl.ds(st, tq), :]

    def update(j, carry):
        m, l, acc = carry
        s, vb = scores(j)
        m_new = jnp.maximum(m, jnp.max(s, axis=1, keepdims=True))
        alpha = jnp.exp2(m - m_new)
        p = jnp.exp2(s - m_new)
        l = alpha * l + jnp.sum(p, axis=1, keepdims=True)
        acc = alpha * acc + jnp.dot(p.astype(BF16), vb, preferred_element_type=F32)
        return m_new, l, acc

    s_own, v_own = scores(i)
    causal = lax.broadcasted_iota(jnp.int32, (2 * tq, tq), 1) <= (
        lax.broadcasted_iota(jnp.int32, (2 * tq, tq), 0) & (tq - 1))
    s_own = jnp.where(causal, s_own, NEG_INF)
    m0 = jnp.max(s_own, axis=1, keepdims=True)
    p = jnp.exp2(s_own - m0)
    carry = (m0, jnp.sum(p, axis=1, keepdims=True), jnp.dot(p.astype(BF16), v_own, preferred_element_type=F32))
    _, l, acc = lax.fori_loop(0, i, update, carry)
    out = acc / l
    o_ref[0] = jnp.where(lane < HEAD_DIM, out[:tq], out[tq:])


def _moba_attention(q, k, v, kmean, tq=ATTN_TILE):
    B, S, _ = q.shape
    nkb = S // MOBA_BLOCK
    grid = (B, PAIRS, S // tq)
    return pl.pallas_call(
        functools.partial(_moba_kernel, nkb=nkb, tq=tq), grid=grid,
        in_specs=[pl.BlockSpec((1, tq, LANES), lambda b, p, i: (b, i, p)),
                  pl.BlockSpec((1, S, LANES), lambda b, p, i: (b, 0, p)),
                  pl.BlockSpec((1, S, LANES), lambda b, p, i: (b, 0, p)),
                  pl.BlockSpec((1, nkb, LANES), lambda b, p, i: (b, 0, p))],
        out_specs=pl.BlockSpec((1, tq, LANES), lambda b, p, i: (b, i, p)),
        out_shape=jax.ShapeDtypeStruct((B, S, D_INNER), F32),
        compiler_params=_cparams(3), name="moba_attn")(q, k, v, kmean)


def _band_kernel(*refs, max_back, kv_heads, use_sinks, want_lse):
    tq = BAND_BLOCK
    pos = 0
    if use_sinks:
        sink_ref = refs[0]
        pos = 1
    q_ref, kp_ref, kc_ref, vp_ref, vc_ref = refs[pos:pos + 5]
    o_ref = refs[pos + 5]
    lse_ref = refs[pos + 6] if want_lse else None
    i = pl.program_id(2)
    rep = N_HEADS // kv_heads

    lane = lax.broadcasted_iota(jnp.int32, (tq, LANES), 1)
    row = lax.broadcasted_iota(jnp.int32, (tq, 2 * tq), 0)
    col = lax.broadcasted_iota(jnp.int32, (tq, 2 * tq), 1)
    dist = row - col + tq
    first_key = jnp.where(i > 0, 0, tq)
    valid = (dist >= 0) & (dist <= max_back) & (col >= first_key)
    low_half = lane < HEAD_DIM

    kcat = jnp.concatenate([kp_ref[0], kc_ref[0]], axis=0)
    vcat = jnp.concatenate([vp_ref[0], vc_ref[0]], axis=0)
    lse_acc = jnp.zeros((tq, LANES), F32)
    for pr in range(PAIRS):
        qf = q_ref[0, :, pr * LANES:(pr + 1) * LANES].astype(F32)
        q_same = qf.astype(BF16)
        q_swap = pltpu.roll(qf, HEAD_DIM, 1).astype(BF16)
        outs = []
        for hh in range(2):
            h = 2 * pr + hh
            g = h // rep
            gh = g % 2
            qsrc = q_same if gh == hh else q_swap
            qm = jnp.where(low_half == (gh == 0), qsrc, jnp.zeros_like(qsrc))
            kt = kcat[:, (g // 2) * LANES:(g // 2 + 1) * LANES]
            vt = vcat[:, (g // 2) * LANES:(g // 2 + 1) * LANES]
            s = jnp.where(valid, _nt_dot(qm, kt), NEG_INF)
            m = jnp.max(s, axis=1, keepdims=True)
            if use_sinks:
                sink = sink_ref[h]
                m = jnp.maximum(m, sink)
            e = jnp.exp(s - m)
            den = jnp.sum(e, axis=1, keepdims=True)
            if use_sinks:
                den = den + jnp.exp(sink - m)
            o = jnp.dot(e.astype(BF16), vt, preferred_element_type=F32) / den
            if gh != hh:
                o = pltpu.roll(o, HEAD_DIM, 1)
            outs.append(o)
            if want_lse:
                lse_acc = jnp.where(lane == h, m + jnp.log(den), lse_acc)
        o_ref[0, :, pr * LANES:(pr + 1) * LANES] = jnp.where(low_half, outs[0], outs[1])
    if want_lse:
        lse_ref[0] = lse_acc


def _band_attention(q, k, v, *, dilation, max_back, kv_heads, sinks=None, want_lse=False):
    B, S, _ = q.shape
    L = S // dilation
    kvw = kv_heads * HEAD_DIM
    tq = BAND_BLOCK
    qv = q.reshape(B, L, dilation * D_INNER)
    kv_ = k.reshape(B, L, dilation * kvw)
    vv = v.reshape(B, L, dilation * kvw)
    grid = (B, dilation, L // tq)
    cur = lambda b, r, i: (b, i, r)
    prev = lambda b, r, i: (b, jnp.maximum(i - 1, 0), r)
    in_specs = [pl.BlockSpec((1, tq, D_INNER), cur),
                pl.BlockSpec((1, tq, kvw), prev), pl.BlockSpec((1, tq, kvw), cur),
                pl.BlockSpec((1, tq, kvw), prev), pl.BlockSpec((1, tq, kvw), cur)]
    args = [qv, kv_, kv_, vv, vv]
    if sinks is not None:
        in_specs = [pl.BlockSpec(memory_space=pltpu.SMEM)] + in_specs
        args = [sinks] + args
    out_shape = [jax.ShapeDtypeStruct((B, L, dilation * D_INNER), F32)]
    out_specs = [pl.BlockSpec((1, tq, D_INNER), cur)]
    if want_lse:
        out_shape.append(jax.ShapeDtypeStruct((B, L, dilation * LANES), F32))
        out_specs.append(pl.BlockSpec((1, tq, LANES), cur))
    kern = functools.partial(_band_kernel, max_back=max_back, kv_heads=kv_heads,
                             use_sinks=sinks is not None, want_lse=want_lse)
    res = pl.pallas_call(kern, grid=grid, in_specs=in_specs, out_specs=out_specs, out_shape=out_shape,
                         compiler_params=_cparams(3), name="band_attn")(*args)
    o = res[0].reshape(B, S, D_INNER)
    if want_lse:
        return o, res[1].reshape(B, S, LANES)
    return o


def _out_kernel(*refs, n_groups):
    ys = refs[:n_groups]
    pos = n_groups
    if n_groups > 1:
        lses = refs[pos:pos + n_groups]
        expand_ref = refs[pos + n_groups]
        pos += n_groups + 1
    z_ref, x_ref, w_ref, g_ref, b_ref, o_ref = refs[pos:pos + 6]
    if n_groups == 1:
        y = ys[0][...]
    else:
        ls = [r[...] for r in lses]
        mx = functools.reduce(jnp.maximum, ls)
        es = [jnp.exp(l - mx) for l in ls]
        tot = functools.reduce(lambda a, b: a + b, es)
        y = None
        for e, yr in zip(es, ys):
            wts = jnp.dot(e / tot, expand_ref[...], precision=lax.Precision.HIGHEST,
                          preferred_element_type=F32)
            y = wts * yr[...] if y is None else y + wts * yr[...]
    z = z_ref[...]
    u = (y * (z * jax.nn.sigmoid(z))).astype(BF16)
    r = DN_ALPHA * x_ref[...] + jnp.dot(u, w_ref[...], preferred_element_type=F32)
    mu = jnp.mean(r, axis=1, keepdims=True)
    d = r - mu
    var = jnp.mean(d * d, axis=1, keepdims=True)
    o_ref[...] = d * lax.rsqrt(var + LN_EPS) * g_ref[...] + b_ref[...]


def _out_block(ys, lses, z, x2, w_out_bf, g, b, tm=256):
    T, D = x2.shape
    n_groups = len(ys)
    row = pl.BlockSpec((tm, D), lambda i: (i, 0))
    in_specs = [row] * n_groups
    args = list(ys)
    if n_groups > 1:
        in_specs += [pl.BlockSpec((tm, LANES), lambda i: (i, 0))] * n_groups
        args += list(lses)
        expand = (np.arange(LANES)[:, None] == (np.arange(D_INNER) // HEAD_DIM)[None, :]).astype(np.float32)
        in_specs.append(pl.BlockSpec((LANES, D_INNER), lambda i: (0, 0)))
        args.append(jnp.asarray(expand))
    in_specs += [row, row, pl.BlockSpec((D_INNER, D), lambda i: (0, 0)),
                 pl.BlockSpec((1, D), lambda i: (0, 0)), pl.BlockSpec((1, D), lambda i: (0, 0))]
    args += [z, x2, w_out_bf, g.reshape(1, D), b.reshape(1, D)]
    return pl.pallas_call(
        functools.partial(_out_kernel, n_groups=n_groups), grid=(T // tm,),
        in_specs=in_specs, out_specs=row, out_shape=jax.ShapeDtypeStruct((T, D), F32),
        compiler_params=_cparams(1), name="out_ln")(*args)


def kernel(x, sb_w_in, sb_w_out, ln0_g, ln0_b, moba_w_in, moba_w_out, ln1_g, ln1_b,
           swa_w_in, swa_sinks, swa_w_out, ln2_g, ln2_b, dil_w_in, dil_w_out, ln3_g, ln3_b):
    B, S, D = x.shape
    T = B * S
    x2 = x.reshape(T, D)
    rope = _rope_lane_tables(S)
    W = D_INNER

    segs = ((0, W, False, Q_SCALE_LOG2, 0), (W, W, False, 1.0, 1), (2 * W, W, False, 1.0, 2),
            (3 * W, W, False, 1.0, 3))
    q, k, v, z = _project(x2, sb_w_in.astype(BF16), segs, (W, W, W, W), (BF16, BF16, BF16, F32))
    y = _sb_attention(q.reshape(B, S, W), k.reshape(B, S, W), v.reshape(B, S, W))
    x2 = _out_block([y.reshape(T, W)], None, z, x2, sb_w_out.astype(BF16), ln0_g, ln0_b)

    segs = ((0, W, True, Q_SCALE_LOG2, 0), (W, W, True, 1.0, 1), (2 * W, W, False, 1.0, 2),
            (3 * W, W, False, 1.0, 3))
    q, k, v, z, kmean = _project(x2, moba_w_in.astype(BF16), segs, (W, W, W, W), (BF16, BF16, BF16, F32),
                                 rope_tabs=rope, kmean_seg=1)
    y = _moba_attention(q.reshape(B, S, W), k.reshape(B, S, W), v.reshape(B, S, W),
                        kmean.reshape(B, S // MOBA_BLOCK, W))
    x2 = _out_block([y.reshape(T, W)], None, z, x2, moba_w_out.astype(BF16), ln1_g, ln1_b)

    kvw = SWA_KV_HEADS * HEAD_DIM
    segs = ((0, W, True, Q_SCALE, 0), (W, kvw, True, 1.0, 1), (W + kvw, kvw, False, 1.0, 2),
            (W + 2 * kvw, W, False, 1.0, 3))
    q, k, v, z = _project(x2, swa_w_in.astype(BF16), segs, (W, kvw, kvw, W), (BF16, BF16, BF16, F32),
                          rope_tabs=rope)
    y = _band_attention(q.reshape(B, S, W), k.reshape(B, S, kvw), v.reshape(B, S, kvw), dilation=1,
                        max_back=SWA_WINDOW - 1, kv_heads=SWA_KV_HEADS, sinks=swa_sinks.astype(F32))
    x2 = _out_block([y.reshape(T, W)], None, z, x2, swa_w_out.astype(BF16), ln2_g, ln2_b)

    n_g = len(DILATED_GROUPS)
    segs = []
    for g in range(n_g):
        segs += [((3 * g) * W, W, True, Q_SCALE, 3 * g), ((3 * g + 1) * W, W, True, 1.0, 3 * g + 1),
                 ((3 * g + 2) * W, W, False, 1.0, 3 * g + 2)]
    segs.append((3 * n_g * W, W, False, 1.0, 3 * n_g))
    outs = _project(x2, dil_w_in.astype(BF16), tuple(segs), (W,) * (3 * n_g + 1), (BF16,) * (3 * n_g) + (F32,),
                    rope_tabs=rope)
    z = outs[-1]
    ys, lses = [], []
    for g, (window, dil) in enumerate(DILATED_GROUPS):
        qg, kg, vg = (outs[3 * g + t].reshape(B, S, W) for t in range(3))
        o, lse = _band_attention(qg, kg, vg, dilation=dil, max_back=window // dil, kv_heads=N_HEADS,
                                 want_lse=True)
        ys.append(o.reshape(T, W))
        lses.append(lse.reshape(T, LANES))
    x2 = _out_block(ys, lses, z, x2, dil_w_out.astype(BF16), ln3_g, ln3_b)
    return x2.reshape(B, S, D)
```

```python
import functools

import jax
import jax.numpy as jnp
import numpy as np
from jax import lax
from jax.experimental import pallas as pl
from jax.experimental.pallas import tpu as pltpu

D_MODEL = 1024
HEAD_DIM = 64
N_HEADS = D_MODEL // HEAD_DIM
D_INNER = N_HEADS * HEAD_DIM
ROPE_THETA = 500000.0
ROT_DIM = HEAD_DIM // 4
LN_EPS = 1e-5
DEPTH = 4
DN_ALPHA = (2.0 * DEPTH) ** 0.25
MOBA_BLOCK = 256
MOBA_TOPK = 3
SWA_WINDOW = 128
SWA_KV_HEADS = 4
DILATED_GROUPS = ((128, 1), (512, 4), (2048, 16))
BAND_BLOCK = 128
ATTN_TILE = 512
SB_SUB = 256
Q_SCALE = HEAD_DIM ** -0.5
Q_SCALE_LOG2 = Q_SCALE * float(np.log2(np.e))

LANES = 128
PAIRS = D_INNER // LANES
VMEM_LIMIT = 56 * 1024 * 1024

F32 = jnp.float32
BF16 = jnp.bfloat16
NEG_INF = float("-inf")
MASKED = -2.0 ** 60


def _cparams(n_axes):
    return pltpu.CompilerParams(dimension_semantics=("arbitrary",) * n_axes, vmem_limit_bytes=VMEM_LIMIT)


def _nt_dot(a, b):
    return lax.dot_general(a, b, (((1,), (1,)), ((), ())), preferred_element_type=F32)


def _rope_lane_tables(seq_len):
    half = ROT_DIM // 2
    pos = jnp.arange(seq_len, dtype=F32)
    inv = ROPE_THETA ** (-jnp.arange(0, ROT_DIM, 2, dtype=F32) / ROT_DIM)
    ang = pos[:, None] * inv[None, :]
    cos, sin = jnp.cos(ang), jnp.sin(ang)
    hl = np.arange(LANES) % HEAD_DIM
    idx = jnp.asarray(hl % half)
    cos_l, sin_l = cos[:, idx], sin[:, idx]
    c = jnp.where(jnp.asarray(hl < ROT_DIM)[None, :], cos_l, 1.0)
    s1 = jnp.where(jnp.asarray(hl < half)[None, :], -sin_l, 0.0)
    s2 = jnp.where(jnp.asarray((hl >= half) & (hl < ROT_DIM))[None, :], sin_l, 0.0)
    return c.astype(F32), s1.astype(F32), s2.astype(F32)


def _apply_rope(y, c, s1, s2):
    parts = []
    for t in range(y.shape[1] // LANES):
        yt = y[:, t * LANES:(t + 1) * LANES]
        up = pltpu.roll(yt, LANES - ROT_DIM // 2, 1)
        dn = pltpu.roll(yt, ROT_DIM // 2, 1)
        parts.append(yt * c + up * s1 + dn * s2)
    return parts[0] if len(parts) == 1 else jnp.concatenate(parts, axis=1)


def _proj_kernel(*refs, segs, use_rope, kmean_seg, out_dils):
    x_ref, w_ref = refs[0], refs[1]
    pos = 2
    if use_rope:
        c, s1, s2 = refs[2][...], refs[3][...], refs[4][...]
        pos = 5
    streamed = any(d > 1 for d in out_dils)
    outs = refs[pos:-1] if streamed else refs[pos:]
    xb = x_ref[...].astype(BF16)
    for si, (col0, width, rope, scale, oi) in enumerate(segs):
        y = jnp.dot(xb, w_ref[:, col0:col0 + width], preferred_element_type=F32)
        if rope:
            y = _apply_rope(y, c, s1, s2)
        if scale != 1.0:
            y = y * scale
        d = out_dils[oi]
        if d == 1:
            outs[oi][...] = y.astype(outs[oi].dtype)
        else:
            y_scr = refs[-1]
            rows = y.shape[0] // d
            for t in range(width // LANES):
                y_scr[t] = y[:, t * LANES:(t + 1) * LANES]
            for r in range(d):
                for t in range(width // LANES):
                    lanes = slice(r * width + t * LANES, r * width + (t + 1) * LANES)
                    outs[oi][:, lanes] = y_scr[t, pl.ds(r, rows, stride=d), :].astype(outs[oi].dtype)
        if kmean_seg == si:
            tm = y.shape[0]
            km = outs[-1]
            for blk in range(tm // MOBA_BLOCK):
                rows = y[blk * MOBA_BLOCK:(blk + 1) * MOBA_BLOCK, :]
                km[0, blk:blk + 1, :] = jnp.sum(rows, axis=0, keepdims=True) * (1.0 / MOBA_BLOCK)


def _project(x2, w_bf, segs, out_widths, out_dtypes, rope_tabs=None, kmean_seg=None, out_dils=None, tm=256):
    T, D = x2.shape
    N = w_bf.shape[1]
    out_dils = out_dils or (1,) * len(out_widths)
    seq = rope_tabs[0].shape[0] if rope_tabs is not None else None
    in_specs = [pl.BlockSpec((tm, D), lambda i: (i, 0)),
                pl.BlockSpec((D, N), lambda i: (0, 0), pipeline_mode=pl.Buffered(1))]
    args = [x2, w_bf]
    if rope_tabs is not None:
        nblk = seq // tm
        for t in rope_tabs:
            in_specs.append(pl.BlockSpec((tm, LANES), lambda i: (i % nblk, 0)))
            args.append(t)
    out_shape = [jax.ShapeDtypeStruct((T // d, d * w), dt) for w, dt, d in zip(out_widths, out_dtypes, out_dils)]
    out_specs = [pl.BlockSpec((tm // d, d * w), lambda i: (i, 0)) for w, d in zip(out_widths, out_dils)]
    if kmean_seg is not None:
        nb = tm // MOBA_BLOCK
        out_shape.append(jax.ShapeDtypeStruct((T // tm, nb, D_INNER), F32))
        out_specs.append(pl.BlockSpec((1, nb, D_INNER), lambda i: (i, 0, 0)))
    scratch = [pltpu.VMEM((max(out_widths) // LANES, tm, LANES), F32)] if any(d > 1 for d in out_dils) else []
    kern = functools.partial(_proj_kernel, segs=segs, use_rope=rope_tabs is not None, kmean_seg=kmean_seg,
                             out_dils=out_dils)
    return pl.pallas_call(
        kern, grid=(T // tm,), in_specs=in_specs, out_specs=out_specs, out_shape=out_shape,
        scratch_shapes=scratch, compiler_params=_cparams(1), name="in_proj")(*args)


def _sb_kernel(q_ref, k_ref, v_ref, o_ref, g_scr, *, tq):
    i = pl.program_id(2)
    q = q_ref[0]
    lane = lax.broadcasted_iota(jnp.int32, (tq, LANES), 1)
    zero = jnp.zeros_like(q)
    q2 = jnp.concatenate([jnp.where(lane < HEAD_DIM, q, zero), jnp.where(lane >= HEAD_DIM, q, zero)], axis=0)
    sub = SB_SUB
    row = lax.broadcasted_iota(jnp.int32, (sub, sub), 0)
    col = lax.broadcasted_iota(jnp.int32, (sub, sub), 1)
    suffix = jnp.where(row > col, 1.0, 0.0).astype(BF16)

    n_sub = tq // sub

    def weigh(j, slot, diag):
        start = pl.multiple_of(j * tq, tq)
        z = _nt_dot(q2, k_ref[0, pl.ds(start, tq), :])
        neg_abs = lax.bitcast_convert_type(lax.bitcast_convert_type(z, jnp.int32) | jnp.int32(-2 ** 31), F32)
        log_beta = jnp.minimum(z, 0.0) - jnp.log2(1.0 + jnp.exp2(neg_abs))
        l1m = log_beta - z
        if diag:
            qrow = lax.broadcasted_iota(jnp.int32, (2 * tq, tq), 0) & (tq - 1)
            before = lax.broadcasted_iota(jnp.int32, (2 * tq, tq), 1) < qrow
            l1m = jnp.where(before, l1m, 0.0)
            log_beta = jnp.where(before, log_beta, NEG_INF)
        lb = l1m.astype(BF16)
        sums = []
        for s in range(n_sub):
            blk = slice(s * sub, (s + 1) * sub)
            g_scr[slot, :, blk] = log_beta[:, blk] + jnp.dot(lb[:, blk], suffix, preferred_element_type=F32)
            sums.append(jnp.sum(l1m[:, blk], axis=1, keepdims=True))
        return tuple(sums)

    def gather(j, slot, sums, c, acc):
        start = pl.multiple_of(j * tq, tq)
        parts = [None] * n_sub
        for s in reversed(range(n_sub)):
            parts[s] = jnp.exp2(g_scr[slot, :, s * sub:(s + 1) * sub] + c).astype(BF16)
            c = c + sums[s]
        a = jnp.concatenate(parts, axis=1)
        acc = acc + jnp.dot(a, v_ref[0, pl.ds(start, tq), :], preferred_element_type=F32)
        return c, acc

    odd = i % 2
    sums = weigh(i, odd, True)
    carry = (sums, jnp.zeros((2 * tq, 1), F32), jnp.zeros((2 * tq, LANES), F32))

    def single(n, cr):
        c1, a1 = gather(i, 1, cr[0], cr[1], cr[2])
        return weigh(i - 1, 0, False), c1, a1

    def double(n, cr):
        j = i - odd - 2 * n
        c1, a1 = gather(j, 0, cr[0], cr[1], cr[2])
        s1 = weigh(j - 1, 1, False)
        c2, a2 = gather(j - 1, 1, s1, c1, a1)
        return weigh(j - 2, 0, False), c2, a2

    carry = lax.fori_loop(0, odd, single, carry)
    sums, c, acc = lax.fori_loop(0, (i - odd) // 2, double, carry)
    _, acc = gather(0, 0, sums, c, acc)
    o_ref[0] = jnp.where(lane < HEAD_DIM, acc[:tq], acc[tq:])


def _sb_attention(q, k, v, tq=ATTN_TILE):
    B, S, _ = q.shape
    grid = (B, PAIRS, S // tq)
    return pl.pallas_call(
        functools.partial(_sb_kernel, tq=tq), grid=grid,
        in_specs=[pl.BlockSpec((1, tq, LANES), lambda b, p, i: (b, i, p)),
                  pl.BlockSpec((1, S, LANES), lambda b, p, i: (b, 0, p)),
                  pl.BlockSpec((1, S, LANES), lambda b, p, i: (b, 0, p))],
        out_specs=pl.BlockSpec((1, tq, LANES), lambda b, p, i: (b, i, p)),
        out_shape=jax.ShapeDtypeStruct((B, S, D_INNER), F32),
        scratch_shapes=[pltpu.VMEM((2, 2 * tq, tq), F32)],
        compiler_params=_cparams(3), name="sb_attn")(q, k, v)


def _moba_kernel(q_ref, k_ref, v_ref, km_ref, o_ref, s_scr, *, nkb, tq):
    i = pl.program_id(2)
    q = q_ref[0]
    km = km_ref[0]
    lane = lax.broadcasted_iota(jnp.int32, (tq, LANES), 1)
    per_tile = tq // MOBA_BLOCK
    blk = lax.broadcasted_iota(jnp.int32, (2 * tq, nkb), 1)
    blk_f = blk.astype(F32)
    qrow = lax.broadcasted_iota(jnp.int32, (2 * tq, nkb), 0) & (tq - 1)
    q_blk = i * per_tile + qrow // MOBA_BLOCK
    past = blk < q_blk

    zero = jnp.zeros_like(q)
    q2 = jnp.concatenate([jnp.where(lane < HEAD_DIM, q, zero), jnp.where(lane >= HEAD_DIM, q, zero)], axis=0)
    km_hi = km.astype(BF16)
    km_lo = (km - km_hi.astype(F32)).astype(BF16)
    gate = _nt_dot(jnp.concatenate([q2, q2], axis=1), jnp.concatenate([km_hi, km_lo], axis=1))
    g = jnp.where(past, gate, NEG_INF)
    sel = jnp.zeros((2 * tq, nkb), jnp.bool_)
    for _ in range(MOBA_TOPK):
        mx = jnp.max(g, axis=1, keepdims=True)
        first = jnp.min(jnp.where(g == mx, blk_f, float(nkb)), axis=1, keepdims=True)
        pick = blk_f == first
        sel = sel | (pick & past)
        g = jnp.where(pick, NEG_INF, g)
    bias = jnp.where(sel | (blk == q_blk), 0.0, MASKED).astype(BF16)
    pad = jnp.zeros((2 * tq, LANES - nkb), BF16)
    qx = jnp.concatenate([q2, bias, pad], axis=1)
    lane_k = lax.broadcasted_iota(jnp.int32, (tq, LANES), 1)
    key_blk = lax.broadcasted_iota(jnp.int32, (tq, LANES), 0) // MOBA_BLOCK

    def score(j, slot, diag):
        st = pl.multiple_of(j * tq, tq)
        onehot = jnp.where(lane_k == j * per_tile + key_blk, 1.0, 0.0).astype(BF16)
        kx = jnp.concatenate([k_ref[0, pl.ds(st, tq), :], onehot], axis=1)
        s = _nt_dot(qx, kx)
        if diag:
            causal = lax.broadcasted_iota(jnp.int32, (2 * tq, tq), 1) <= (
                lax.broadcasted_iota(jnp.int32, (2 * tq, tq), 0) & (tq - 1))
            s = jnp.where(causal, s, NEG_INF)
        s_scr[slot] = s
        return jnp.max(s, axis=1, keepdims=True)

    def absorb(j, slot, m_tile, carry):
        m, l, acc = carry
        st = pl.multiple_of(j * tq, tq)
        m_new = jnp.maximum(m, m_tile)
        alpha = jnp.exp2(m - m_new)
        p = jnp.exp2(s_scr[slot] - m_new)
        l = alpha * l + jnp.sum(p, axis=1, keepdims=True)
        acc = alpha * acc + jnp.dot(p.astype(BF16), v_ref[0, pl.ds(st, tq), :], preferred_element_type=F32)
        return m_new, l, acc

    odd = i % 2
    m_tile = score(i, odd, True)
    state = (jnp.full((2 * tq, 1), NEG_INF, F32), jnp.zeros((2 * tq, 1), F32), jnp.zeros((2 * tq, LANES), F32))

    def single(n, cr):
        st1 = absorb(i, 1, cr[0], cr[1])
        return score(i - 1, 0, False), st1

    def double(n, cr):
        j = i - odd - 2 * n
        st1 = absorb(j, 0, cr[0], cr[1])
        m1 = score(j - 1, 1, False)
        st2 = absorb(j - 1, 1, m1, st1)
        return score(j - 2, 0, False), st2

    carry = lax.fori_loop(0, odd, single, (m_tile, state))
    m_tile, state = lax.fori_loop(0, (i - odd) // 2, double, carry)
    _, l, acc = absorb(0, 0, m_tile, state)
    out = acc / l
    o_ref[0] = jnp.where(lane < HEAD_DIM, out[:tq], out[tq:])


def _moba_attention(q, k, v, kmean, tq=ATTN_TILE):
    B, S, _ = q.shape
    nkb = S // MOBA_BLOCK
    grid = (B, PAIRS, S // tq)
    return pl.pallas_call(
        functools.partial(_moba_kernel, nkb=nkb, tq=tq), grid=grid,
        in_specs=[pl.BlockSpec((1, tq, LANES), lambda b, p, i: (b, i, p)),
                  pl.BlockSpec((1, S, LANES), lambda b, p, i: (b, 0, p)),
                  pl.BlockSpec((1, S, LANES), lambda b, p, i: (b, 0, p)),
                  pl.BlockSpec((1, nkb, LANES), lambda b, p, i: (b, 0, p))],
        out_specs=pl.BlockSpec((1, tq, LANES), lambda b, p, i: (b, i, p)),
        out_shape=jax.ShapeDtypeStruct((B, S, D_INNER), F32),
        scratch_shapes=[pltpu.VMEM((2, 2 * tq, tq), F32)],
        compiler_params=_cparams(3), name="moba_attn")(q, k, v, kmean)


def _band_kernel(*refs, max_back, kv_heads, use_sinks, want_lse):
    tq = BAND_BLOCK
    pos = 0
    if use_sinks:
        sink_ref = refs[0]
        pos = 1
    q_ref, kp_ref, kc_ref, vp_ref, vc_ref = refs[pos:pos + 5]
    o_ref = refs[pos + 5]
    lse_ref = refs[pos + 6] if want_lse else None
    i = pl.program_id(2)
    rep = N_HEADS // kv_heads

    lane = lax.broadcasted_iota(jnp.int32, (tq, LANES), 1)
    row = lax.broadcasted_iota(jnp.int32, (tq, 2 * tq), 0)
    col = lax.broadcasted_iota(jnp.int32, (tq, 2 * tq), 1)
    dist = row - col + tq
    first_key = jnp.where(i > 0, 0, tq)
    valid = (dist >= 0) & (dist <= max_back) & (col >= first_key)
    low_half = lane < HEAD_DIM

    kcat = jnp.concatenate([kp_ref[0], kc_ref[0]], axis=0)
    vcat = jnp.concatenate([vp_ref[0], vc_ref[0]], axis=0)
    lse_acc = jnp.zeros((tq, LANES), F32)
    for pr in range(PAIRS):
        qf = q_ref[0, :, pr * LANES:(pr + 1) * LANES].astype(F32)
        q_same = qf.astype(BF16)
        q_swap = pltpu.roll(qf, HEAD_DIM, 1).astype(BF16)
        outs = []
        for hh in range(2):
            h = 2 * pr + hh
            g = h // rep
            gh = g % 2
            qsrc = q_same if gh == hh else q_swap
            qm = jnp.where(low_half == (gh == 0), qsrc, jnp.zeros_like(qsrc))
            kt = kcat[:, (g // 2) * LANES:(g // 2 + 1) * LANES]
            vt = vcat[:, (g // 2) * LANES:(g // 2 + 1) * LANES]
            s = jnp.where(valid, _nt_dot(qm, kt), NEG_INF)
            m = jnp.max(s, axis=1, keepdims=True)
            if use_sinks:
                sink = sink_ref[h]
                m = jnp.maximum(m, sink)
            e = jnp.exp(s - m)
            den = jnp.sum(e, axis=1, keepdims=True)
            if use_sinks:
                den = den + jnp.exp(sink - m)
            o = jnp.dot(e.astype(BF16), vt, preferred_element_type=F32) / den
            if gh != hh:
                o = pltpu.roll(o, HEAD_DIM, 1)
            outs.append(o)
            if want_lse:
                lse_acc = jnp.where(lane == h, m + jnp.log(den), lse_acc)
        o_ref[0, :, pr * LANES:(pr + 1) * LANES] = jnp.where(low_half, outs[0], outs[1])
    if want_lse:
        lse_ref[0] = lse_acc


def _band_attention(q, k, v, *, dilation, max_back, kv_heads, sinks=None, want_lse=False):
    B, L, _ = q.shape
    kvw = kv_heads * HEAD_DIM
    tq = BAND_BLOCK
    qv, kv_, vv = q, k, v
    grid = (B, dilation, L // tq)
    cur = lambda b, r, i: (b, i, r)
    prev = lambda b, r, i: (b, jnp.maximum(i - 1, 0), r)
    in_specs = [pl.BlockSpec((1, tq, D_INNER), cur),
                pl.BlockSpec((1, tq, kvw), prev), pl.BlockSpec((1, tq, kvw), cur),
                pl.BlockSpec((1, tq, kvw), prev), pl.BlockSpec((1, tq, kvw), cur)]
    args = [qv, kv_, kv_, vv, vv]
    if sinks is not None:
        in_specs = [pl.BlockSpec(memory_space=pltpu.SMEM)] + in_specs
        args = [sinks] + args
    out_shape = [jax.ShapeDtypeStruct((B, L, dilation * D_INNER), F32)]
    out_specs = [pl.BlockSpec((1, tq, D_INNER), cur)]
    if want_lse:
        out_shape.append(jax.ShapeDtypeStruct((B, L, dilation * LANES), F32))
        out_specs.append(pl.BlockSpec((1, tq, LANES), cur))
    kern = functools.partial(_band_kernel, max_back=max_back, kv_heads=kv_heads,
                             use_sinks=sinks is not None, want_lse=want_lse)
    res = pl.pallas_call(kern, grid=grid, in_specs=in_specs, out_specs=out_specs, out_shape=out_shape,
                         compiler_params=_cparams(3), name="band_attn")(*args)
    return tuple(res) if want_lse else res[0]


def _out_kernel(*refs, n_groups, dils):
    ys = refs[:n_groups]
    pos = n_groups
    if n_groups > 1:
        lses = refs[pos:pos + n_groups]
        expand_ref = refs[pos + n_groups]
        pos += n_groups + 1
    z_ref, x_ref, w_ref, g_ref, b_ref, o_ref = refs[pos:pos + 6]
    y_scr, l_scr = refs[pos + 6:pos + 8] if any(d > 1 for d in dils) else (None, None)

    def token_order(ref, scr, d, width):
        if d == 1:
            return ref[...]
        rows = ref.shape[0]
        tiles = width // LANES
        for r in range(d):
            for t in range(tiles):
                scr[t, pl.ds(r, rows, stride=d), :] = ref[:, r * width + t * LANES:r * width + (t + 1) * LANES]
        return jnp.concatenate([scr[t] for t in range(tiles)], axis=1) if tiles > 1 else scr[0]

    if n_groups == 1:
        y = ys[0][...]
    else:
        ls = [token_order(r, l_scr, d, LANES) for r, d in zip(lses, dils)]
        mx = functools.reduce(jnp.maximum, ls)
        es = [jnp.exp(l - mx) for l in ls]
        tot = functools.reduce(lambda a, b: a + b, es)
        y = None
        for e, yr, d in zip(es, ys, dils):
            wts = jnp.dot(e / tot, expand_ref[...], precision=lax.Precision.HIGHEST,
                          preferred_element_type=F32)
            yg = wts * token_order(yr, y_scr, d, D_INNER)
            y = yg if y is None else y + yg
    z = z_ref[...]
    u = (y * (z * jax.nn.sigmoid(z))).astype(BF16)
    r = DN_ALPHA * x_ref[...] + jnp.dot(u, w_ref[...], preferred_element_type=F32)
    mu = jnp.mean(r, axis=1, keepdims=True)
    d = r - mu
    var = jnp.mean(d * d, axis=1, keepdims=True)
    o_ref[...] = d * lax.rsqrt(var + LN_EPS) * g_ref[...] + b_ref[...]


def _out_block(ys, lses, z, x2, w_out_bf, g, b, dils=None, tm=256):
    T, D = x2.shape
    n_groups = len(ys)
    dils = dils or (1,) * n_groups
    row = pl.BlockSpec((tm, D), lambda i: (i, 0))
    in_specs = [pl.BlockSpec((tm // d, d * D_INNER), lambda i: (i, 0)) for d in dils]
    args = list(ys)
    if n_groups > 1:
        in_specs += [pl.BlockSpec((tm // d, d * LANES), lambda i: (i, 0)) for d in dils]
        args += list(lses)
        expand = (np.arange(LANES)[:, None] == (np.arange(D_INNER) // HEAD_DIM)[None, :]).astype(np.float32)
        in_specs.append(pl.BlockSpec((LANES, D_INNER), lambda i: (0, 0)))
        args.append(jnp.asarray(expand))
    in_specs += [row, row, pl.BlockSpec((D_INNER, D), lambda i: (0, 0)),
                 pl.BlockSpec((1, D), lambda i: (0, 0)), pl.BlockSpec((1, D), lambda i: (0, 0))]
    args += [z, x2, w_out_bf, g.reshape(1, D), b.reshape(1, D)]
    scratch = ([pltpu.VMEM((D_INNER // LANES, tm, LANES), F32), pltpu.VMEM((1, tm, LANES), F32)]
               if any(d > 1 for d in dils) else [])
    return pl.pallas_call(
        functools.partial(_out_kernel, n_groups=n_groups, dils=dils), grid=(T // tm,),
        in_specs=in_specs, out_specs=row, out_shape=jax.ShapeDtypeStruct((T, D), F32),
        scratch_shapes=scratch, compiler_params=_cparams(1), name="out_ln")(*args)


def kernel(x, sb_w_in, sb_w_out, ln0_g, ln0_b, moba_w_in, moba_w_out, ln1_g, ln1_b,
           swa_w_in, swa_sinks, swa_w_out, ln2_g, ln2_b, dil_w_in, dil_w_out, ln3_g, ln3_b):
    B, S, D = x.shape
    T = B * S
    x2 = x.reshape(T, D)
    rope = _rope_lane_tables(S)
    W = D_INNER

    segs = ((0, W, False, Q_SCALE_LOG2, 0), (W, W, False, 1.0, 1), (2 * W, W, False, 1.0, 2),
            (3 * W, W, False, 1.0, 3))
    q, k, v, z = _project(x2, sb_w_in.astype(BF16), segs, (W, W, W, W), (BF16, BF16, BF16, F32))
    y = _sb_attention(q.reshape(B, S, W), k.reshape(B, S, W), v.reshape(B, S, W))
    x2 = _out_block([y.reshape(T, W)], None, z, x2, sb_w_out.astype(BF16), ln0_g, ln0_b)

    segs = ((0, W, True, Q_SCALE_LOG2, 0), (W, W, True, 1.0, 1), (2 * W, W, False, 1.0, 2),
            (3 * W, W, False, 1.0, 3))
    q, k, v, z, kmean = _project(x2, moba_w_in.astype(BF16), segs, (W, W, W, W), (BF16, BF16, BF16, F32),
                                 rope_tabs=rope, kmean_seg=1)
    y = _moba_attention(q.reshape(B, S, W), k.reshape(B, S, W), v.reshape(B, S, W),
                        kmean.reshape(B, S // MOBA_BLOCK, W))
    x2 = _out_block([y.reshape(T, W)], None, z, x2, moba_w_out.astype(BF16), ln1_g, ln1_b)

    kvw = SWA_KV_HEADS * HEAD_DIM
    segs = ((0, W, True, Q_SCALE, 0), (W, kvw, True, 1.0, 1), (W + kvw, kvw, False, 1.0, 2),
            (W + 2 * kvw, W, False, 1.0, 3))
    q, k, v, z = _project(x2, swa_w_in.astype(BF16), segs, (W, kvw, kvw, W), (BF16, BF16, BF16, F32),
                          rope_tabs=rope)
    y = _band_attention(q.reshape(B, S, W), k.reshape(B, S, kvw), v.reshape(B, S, kvw), dilation=1,
                        max_back=SWA_WINDOW - 1, kv_heads=SWA_KV_HEADS, sinks=swa_sinks.astype(F32))
    x2 = _out_block([y.reshape(T, W)], None, z, x2, swa_w_out.astype(BF16), ln2_g, ln2_b)

    n_g = len(DILATED_GROUPS)
    segs = []
    for g in range(n_g):
        segs += [((3 * g) * W, W, True, Q_SCALE, 3 * g), ((3 * g + 1) * W, W, True, 1.0, 3 * g + 1),
                 ((3 * g + 2) * W, W, False, 1.0, 3 * g + 2)]
    segs.append((3 * n_g * W, W, False, 1.0, 3 * n_g))
    dils = tuple(d for _, d in DILATED_GROUPS)
    out_dils = tuple(d for d in dils for _ in range(3)) + (1,)
    outs = _project(x2, dil_w_in.astype(BF16), tuple(segs), (W,) * (3 * n_g + 1), (BF16,) * (3 * n_g) + (F32,),
                    rope_tabs=rope, out_dils=out_dils)
    z = outs[-1]
    ys, lses = [], []
    for g, (window, dil) in enumerate(DILATED_GROUPS):
        qg, kg, vg = (outs[3 * g + t].reshape(B, S // dil, dil * W) for t in range(3))
        o, lse = _band_attention(qg, kg, vg, dilation=dil, max_back=window // dil, kv_heads=N_HEADS,
                                 want_lse=True)
        ys.append(o.reshape(T // dil, dil * W))
        lses.append(lse.reshape(T // dil, dil * LANES))
    x2 = _out_block(ys, lses, z, x2, dil_w_out.astype(BF16), ln3_g, ln3_b, dils=dils)
    return x2.reshape(B, S, D)
```

```python
import functools

import jax
import jax.numpy as jnp
import numpy as np
from jax import lax
from jax.experimental import pallas as pl
from jax.experimental.pallas import tpu as pltpu

D_MODEL = 1024
HEAD_DIM = 64
N_HEADS = D_MODEL // HEAD_DIM
D_INNER = N_HEADS * HEAD_DIM
ROPE_THETA = 500000.0
ROT_DIM = HEAD_DIM // 4
LN_EPS = 1e-5
DEPTH = 4
DN_ALPHA = (2.0 * DEPTH) ** 0.25
MOBA_BLOCK = 256
MOBA_TOPK = 3
SWA_WINDOW = 128
SWA_KV_HEADS = 4
DILATED_GROUPS = ((128, 1), (512, 4), (2048, 16))
BAND_BLOCK = 128
ATTN_TILE = 512
SB_SUB = 256
Q_SCALE = HEAD_DIM ** -0.5
Q_SCALE_LOG2 = Q_SCALE * float(np.log2(np.e))

LANES = 128
PAIRS = D_INNER // LANES
VMEM_LIMIT = 56 * 1024 * 1024

F32 = jnp.float32
BF16 = jnp.bfloat16
NEG_INF = float("-inf")
MASKED = -2.0 ** 60


def _cparams(n_axes):
    return pltpu.CompilerParams(dimension_semantics=("arbitrary",) * n_axes, vmem_limit_bytes=VMEM_LIMIT)


def _nt_dot(a, b):
    return lax.dot_general(a, b, (((1,), (1,)), ((), ())), preferred_element_type=F32)


def _rope_lane_tables(seq_len):
    half = ROT_DIM // 2
    pos = jnp.arange(seq_len, dtype=F32)
    inv = ROPE_THETA ** (-jnp.arange(0, ROT_DIM, 2, dtype=F32) / ROT_DIM)
    ang = pos[:, None] * inv[None, :]
    cos, sin = jnp.cos(ang), jnp.sin(ang)
    hl = np.arange(LANES) % HEAD_DIM
    idx = jnp.asarray(hl % half)
    cos_l, sin_l = cos[:, idx], sin[:, idx]
    c = jnp.where(jnp.asarray(hl < ROT_DIM)[None, :], cos_l, 1.0)
    s1 = jnp.where(jnp.asarray(hl < half)[None, :], -sin_l, 0.0)
    s2 = jnp.where(jnp.asarray((hl >= half) & (hl < ROT_DIM))[None, :], sin_l, 0.0)
    return c.astype(F32), s1.astype(F32), s2.astype(F32)


def _apply_rope(y, c, s1, s2):
    parts = []
    for t in range(y.shape[1] // LANES):
        yt = y[:, t * LANES:(t + 1) * LANES]
        up = pltpu.roll(yt, LANES - ROT_DIM // 2, 1)
        dn = pltpu.roll(yt, ROT_DIM // 2, 1)
        parts.append(yt * c + up * s1 + dn * s2)
    return parts[0] if len(parts) == 1 else jnp.concatenate(parts, axis=1)


def _proj_kernel(*refs, segs, use_rope, kmean_seg, out_dils):
    x_ref, w_ref = refs[0], refs[1]
    pos = 2
    if use_rope:
        c, s1, s2 = refs[2][...], refs[3][...], refs[4][...]
        pos = 5
    streamed = any(d > 1 for d in out_dils)
    outs = refs[pos:-1] if streamed else refs[pos:]
    xb = x_ref[...].astype(BF16)
    for si, (col0, width, rope, scale, oi) in enumerate(segs):
        y = jnp.dot(xb, w_ref[:, col0:col0 + width], preferred_element_type=F32)
        if rope:
            y = _apply_rope(y, c, s1, s2)
        if scale != 1.0:
            y = y * scale
        d = out_dils[oi]
        if d == 1:
            outs[oi][...] = y.astype(outs[oi].dtype)
        else:
            y_scr = refs[-1]
            rows = y.shape[0] // d
            for t in range(width // LANES):
                y_scr[t] = y[:, t * LANES:(t + 1) * LANES]
            for r in range(d):
                for t in range(width // LANES):
                    lanes = slice(r * width + t * LANES, r * width + (t + 1) * LANES)
                    outs[oi][:, lanes] = y_scr[t, pl.ds(r, rows, stride=d), :].astype(outs[oi].dtype)
        if kmean_seg == si:
            tm = y.shape[0]
            km = outs[-1]
            for blk in range(tm // MOBA_BLOCK):
                rows = y[blk * MOBA_BLOCK:(blk + 1) * MOBA_BLOCK, :]
                km[0, blk:blk + 1, :] = jnp.sum(rows, axis=0, keepdims=True) * (1.0 / MOBA_BLOCK)


def _project(x2, w_bf, segs, out_widths, out_dtypes, rope_tabs=None, kmean_seg=None, out_dils=None, tm=256):
    T, D = x2.shape
    N = w_bf.shape[1]
    out_dils = out_dils or (1,) * len(out_widths)
    seq = rope_tabs[0].shape[0] if rope_tabs is not None else None
    in_specs = [pl.BlockSpec((tm, D), lambda i: (i, 0)),
                pl.BlockSpec((D, N), lambda i: (0, 0), pipeline_mode=pl.Buffered(1))]
    args = [x2, w_bf]
    if rope_tabs is not None:
        nblk = seq // tm
        for t in rope_tabs:
            in_specs.append(pl.BlockSpec((tm, LANES), lambda i: (i % nblk, 0)))
            args.append(t)
    out_shape = [jax.ShapeDtypeStruct((T // d, d * w), dt) for w, dt, d in zip(out_widths, out_dtypes, out_dils)]
    out_specs = [pl.BlockSpec((tm // d, d * w), lambda i: (i, 0)) for w, d in zip(out_widths, out_dils)]
    if kmean_seg is not None:
        nb = tm // MOBA_BLOCK
        out_shape.append(jax.ShapeDtypeStruct((T // tm, nb, D_INNER), F32))
        out_specs.append(pl.BlockSpec((1, nb, D_INNER), lambda i: (i, 0, 0)))
    scratch = [pltpu.VMEM((max(out_widths) // LANES, tm, LANES), F32)] if any(d > 1 for d in out_dils) else []
    kern = functools.partial(_proj_kernel, segs=segs, use_rope=rope_tabs is not None, kmean_seg=kmean_seg,
                             out_dils=out_dils)
    return pl.pallas_call(
        kern, grid=(T // tm,), in_specs=in_specs, out_specs=out_specs, out_shape=out_shape,
        scratch_shapes=scratch, compiler_params=_cparams(1), name="in_proj")(*args)


def _sb_kernel(q_ref, k_ref, v_ref, o_ref, g_scr, *, tq):
    i = pl.program_id(2)
    q = q_ref[0]
    lane = lax.broadcasted_iota(jnp.int32, (tq, LANES), 1)
    zero = jnp.zeros_like(q)
    q2 = jnp.concatenate([jnp.where(lane < HEAD_DIM, q, zero), jnp.where(lane >= HEAD_DIM, q, zero)], axis=0)
    sub = SB_SUB
    row = lax.broadcasted_iota(jnp.int32, (sub, sub), 0)
    col = lax.broadcasted_iota(jnp.int32, (sub, sub), 1)
    suffix = jnp.where(row > col, 1.0, 0.0).astype(BF16)

    n_sub = tq // sub

    def weigh(j, slot, diag):
        start = pl.multiple_of(j * tq, tq)
        z = _nt_dot(q2, k_ref[0, pl.ds(start, tq), :])
        neg_abs = lax.bitcast_convert_type(lax.bitcast_convert_type(z, jnp.int32) | jnp.int32(-2 ** 31), F32)
        log_beta = jnp.minimum(z, 0.0) - jnp.log2(1.0 + jnp.exp2(neg_abs))
        l1m = log_beta - z
        if diag:
            qrow = lax.broadcasted_iota(jnp.int32, (2 * tq, tq), 0) & (tq - 1)
            before = lax.broadcasted_iota(jnp.int32, (2 * tq, tq), 1) < qrow
            l1m = jnp.where(before, l1m, 0.0)
            log_beta = jnp.where(before, log_beta, NEG_INF)
        lb = l1m.astype(BF16)
        sums = []
        for s in range(n_sub):
            blk = slice(s * sub, (s + 1) * sub)
            g_scr[slot, :, blk] = log_beta[:, blk] + jnp.dot(lb[:, blk], suffix, preferred_element_type=F32)
            sums.append(jnp.sum(l1m[:, blk], axis=1, keepdims=True))
        return tuple(sums)

    def gather(j, slot, sums, c, acc):
        start = pl.multiple_of(j * tq, tq)
        parts = [None] * n_sub
        for s in reversed(range(n_sub)):
            parts[s] = jnp.exp2(g_scr[slot, :, s * sub:(s + 1) * sub] + c).astype(BF16)
            c = c + sums[s]
        a = jnp.concatenate(parts, axis=1)
        acc = acc + jnp.dot(a, v_ref[0, pl.ds(start, tq), :], preferred_element_type=F32)
        return c, acc

    odd = i % 2
    sums = weigh(i, odd, True)
    carry = (sums, jnp.zeros((2 * tq, 1), F32), jnp.zeros((2 * tq, LANES), F32))

    def single(n, cr):
        c1, a1 = gather(i, 1, cr[0], cr[1], cr[2])
        return weigh(i - 1, 0, False), c1, a1

    def double(n, cr):
        j = i - odd - 2 * n
        c1, a1 = gather(j, 0, cr[0], cr[1], cr[2])
        s1 = weigh(j - 1, 1, False)
        c2, a2 = gather(j - 1, 1, s1, c1, a1)
        return weigh(j - 2, 0, False), c2, a2

    carry = lax.fori_loop(0, odd, single, carry)
    sums, c, acc = lax.fori_loop(0, (i - odd) // 2, double, carry)
    _, acc = gather(0, 0, sums, c, acc)
    o_ref[0] = jnp.where(lane < HEAD_DIM, acc[:tq], acc[tq:])


def _sb_attention(q, k, v, tq=ATTN_TILE):
    B, S, _ = q.shape
    grid = (B, PAIRS, S // tq)
    return pl.pallas_call(
        functools.partial(_sb_kernel, tq=tq), grid=grid,
        in_specs=[pl.BlockSpec((1, tq, LANES), lambda b, p, i: (b, i, p)),
                  pl.BlockSpec((1, S, LANES), lambda b, p, i: (b, 0, p)),
                  pl.BlockSpec((1, S, LANES), lambda b, p, i: (b, 0, p))],
        out_specs=pl.BlockSpec((1, tq, LANES), lambda b, p, i: (b, i, p)),
        out_shape=jax.ShapeDtypeStruct((B, S, D_INNER), F32),
        scratch_shapes=[pltpu.VMEM((2, 2 * tq, tq), F32)],
        compiler_params=_cparams(3), name="sb_attn")(q, k, v)


def _moba_kernel_t(q_ref, k_ref, v_ref, km_ref, o_ref, s_scr, *, nkb, tq):
    i = pl.program_id(2)
    km = km_ref[0]
    per_tile = tq // MOBA_BLOCK
    dim = lax.broadcasted_iota(jnp.int32, (LANES, tq), 0)
    q_t = q_ref[0].astype(F32).T
    zero = jnp.zeros_like(q_t)
    q2_t = jnp.concatenate([jnp.where(dim < HEAD_DIM, q_t, zero), jnp.where(dim >= HEAD_DIM, q_t, zero)],
                           axis=1).astype(BF16)

    km_hi = km.astype(BF16)
    km_lo = (km - km_hi.astype(F32)).astype(BF16)
    gate = jnp.dot(jnp.concatenate([km_hi, km_lo], axis=1), jnp.concatenate([q2_t, q2_t], axis=0),
                   preferred_element_type=F32)
    blk = lax.broadcasted_iota(jnp.int32, (nkb, 2 * tq), 0)
    blk_f = blk.astype(F32)
    qcol = lax.broadcasted_iota(jnp.int32, (nkb, 2 * tq), 1) & (tq - 1)
    q_blk = i * per_tile + qcol // MOBA_BLOCK
    past = blk < q_blk
    g = jnp.where(past, gate, NEG_INF)
    sel = jnp.zeros((nkb, 2 * tq), jnp.bool_)
    for _ in range(MOBA_TOPK):
        mx = jnp.max(g, axis=0, keepdims=True)
        first = jnp.min(jnp.where(g == mx, blk_f, float(nkb)), axis=0, keepdims=True)
        pick = blk_f == first
        sel = sel | (pick & past)
        g = jnp.where(pick, NEG_INF, g)
    bias = jnp.where(sel | (blk == q_blk), 0.0, MASKED).astype(BF16)
    qx_t = jnp.concatenate([q2_t, bias, jnp.zeros((LANES - nkb, 2 * tq), BF16)], axis=0)
    lane_k = lax.broadcasted_iota(jnp.int32, (tq, LANES), 1)
    key_blk = lax.broadcasted_iota(jnp.int32, (tq, LANES), 0) // MOBA_BLOCK

    def score(j, slot, diag):
        st = pl.multiple_of(j * tq, tq)
        onehot = jnp.where(lane_k == j * per_tile + key_blk, 1.0, 0.0).astype(BF16)
        kx = jnp.concatenate([k_ref[0, pl.ds(st, tq), :], onehot], axis=1)
        s = jnp.dot(kx, qx_t, preferred_element_type=F32)
        if diag:
            causal = lax.broadcasted_iota(jnp.int32, (tq, 2 * tq), 0) <= (
                lax.broadcasted_iota(jnp.int32, (tq, 2 * tq), 1) & (tq - 1))
            s = jnp.where(causal, s, NEG_INF)
        s_scr[slot] = s
        return jnp.max(s, axis=0, keepdims=True)

    def absorb(j, slot, m_tile, carry):
        m, l, acc = carry
        st = pl.multiple_of(j * tq, tq)
        v_t = v_ref[0, pl.ds(st, tq), :].astype(F32).T.astype(BF16)
        m_new = jnp.maximum(m, m_tile)
        alpha = jnp.exp2(m - m_new)
        p = jnp.exp2(s_scr[slot] - m_new)
        l = alpha * l + jnp.sum(p, axis=0, keepdims=True)
        acc = alpha * acc + jnp.dot(v_t, p.astype(BF16), preferred_element_type=F32)
        return m_new, l, acc

    odd = i % 2
    m_tile = score(i, odd, True)
    state = (jnp.full((1, 2 * tq), NEG_INF, F32), jnp.zeros((1, 2 * tq), F32), jnp.zeros((LANES, 2 * tq), F32))

    def single(n, cr):
        st1 = absorb(i, 1, cr[0], cr[1])
        return score(i - 1, 0, False), st1

    def double(n, cr):
        j = i - odd - 2 * n
        st1 = absorb(j, 0, cr[0], cr[1])
        m1 = score(j - 1, 1, False)
        st2 = absorb(j - 1, 1, m1, st1)
        return score(j - 2, 0, False), st2

    carry = lax.fori_loop(0, odd, single, (m_tile, state))
    m_tile, state = lax.fori_loop(0, (i - odd) // 2, double, carry)
    _, l, acc = absorb(0, 0, m_tile, state)
    out = acc / l
    o_ref[0] = jnp.where(dim < HEAD_DIM, out[:, :tq], out[:, tq:]).T


def _moba_attention(q, k, v, kmean, tq=ATTN_TILE):
    B, S, _ = q.shape
    nkb = S // MOBA_BLOCK
    grid = (B, PAIRS, S // tq)
    return pl.pallas_call(
        functools.partial(_moba_kernel_t, nkb=nkb, tq=tq), grid=grid,
        in_specs=[pl.BlockSpec((1, tq, LANES), lambda b, p, i: (b, i, p)),
                  pl.BlockSpec((1, S, LANES), lambda b, p, i: (b, 0, p)),
                  pl.BlockSpec((1, S, LANES), lambda b, p, i: (b, 0, p)),
                  pl.BlockSpec((1, nkb, LANES), lambda b, p, i: (b, 0, p))],
        out_specs=pl.BlockSpec((1, tq, LANES), lambda b, p, i: (b, i, p)),
        out_shape=jax.ShapeDtypeStruct((B, S, D_INNER), F32),
        scratch_shapes=[pltpu.VMEM((2, tq, 2 * tq), F32)],
        compiler_params=_cparams(3), name="moba_attn")(q, k, v, kmean)


def _band_kernel(*refs, max_back, kv_heads, use_sinks, want_lse):
    tq = BAND_BLOCK
    pos = 0
    if use_sinks:
        sink_ref = refs[0]
        pos = 1
    q_ref, kp_ref, kc_ref, vp_ref, vc_ref = refs[pos:pos + 5]
    o_ref = refs[pos + 5]
    lse_ref = refs[pos + 6] if want_lse else None
    i = pl.program_id(2)
    rep = N_HEADS // kv_heads

    lane = lax.broadcasted_iota(jnp.int32, (tq, LANES), 1)
    row = lax.broadcasted_iota(jnp.int32, (tq, 2 * tq), 0)
    col = lax.broadcasted_iota(jnp.int32, (tq, 2 * tq), 1)
    dist = row - col + tq
    first_key = jnp.where(i > 0, 0, tq)
    valid = (dist >= 0) & (dist <= max_back) & (col >= first_key)
    low_half = lane < HEAD_DIM

    kcat = jnp.concatenate([kp_ref[0], kc_ref[0]], axis=0)
    vcat = jnp.concatenate([vp_ref[0], vc_ref[0]], axis=0)
    lse_acc = jnp.zeros((tq, LANES), F32)
    for pr in range(PAIRS):
        qf = q_ref[0, :, pr * LANES:(pr + 1) * LANES].astype(F32)
        q_same = qf.astype(BF16)
        q_swap = pltpu.roll(qf, HEAD_DIM, 1).astype(BF16)
        outs = []
        for hh in range(2):
            h = 2 * pr + hh
            g = h // rep
            gh = g % 2
            qsrc = q_same if gh == hh else q_swap
            qm = jnp.where(low_half == (gh == 0), qsrc, jnp.zeros_like(qsrc))
            kt = kcat[:, (g // 2) * LANES:(g // 2 + 1) * LANES]
            vt = vcat[:, (g // 2) * LANES:(g // 2 + 1) * LANES]
            s = jnp.where(valid, _nt_dot(qm, kt), NEG_INF)
            m = jnp.max(s, axis=1, keepdims=True)
            if use_sinks:
                sink = sink_ref[h]
                m = jnp.maximum(m, sink)
            e = jnp.exp(s - m)
            den = jnp.sum(e, axis=1, keepdims=True)
            if use_sinks:
                den = den + jnp.exp(sink - m)
            o = jnp.dot(e.astype(BF16), vt, preferred_element_type=F32) / den
            if gh != hh:
                o = pltpu.roll(o, HEAD_DIM, 1)
            outs.append(o)
            if want_lse:
                lse_acc = jnp.where(lane == h, m + jnp.log(den), lse_acc)
        o_ref[0, :, pr * LANES:(pr + 1) * LANES] = jnp.where(low_half, outs[0], outs[1])
    if want_lse:
        lse_ref[0] = lse_acc


def _band_attention(q, k, v, *, dilation, max_back, kv_heads, sinks=None, want_lse=False):
    B, L, _ = q.shape
    kvw = kv_heads * HEAD_DIM
    tq = BAND_BLOCK
    qv, kv_, vv = q, k, v
    grid = (B, dilation, L // tq)
    cur = lambda b, r, i: (b, i, r)
    prev = lambda b, r, i: (b, jnp.maximum(i - 1, 0), r)
    in_specs = [pl.BlockSpec((1, tq, D_INNER), cur),
                pl.BlockSpec((1, tq, kvw), prev), pl.BlockSpec((1, tq, kvw), cur),
                pl.BlockSpec((1, tq, kvw), prev), pl.BlockSpec((1, tq, kvw), cur)]
    args = [qv, kv_, kv_, vv, vv]
    if sinks is not None:
        in_specs = [pl.BlockSpec(memory_space=pltpu.SMEM)] + in_specs
        args = [sinks] + args
    out_shape = [jax.ShapeDtypeStruct((B, L, dilation * D_INNER), F32)]
    out_specs = [pl.BlockSpec((1, tq, D_INNER), cur)]
    if want_lse:
        out_shape.append(jax.ShapeDtypeStruct((B, L, dilation * LANES), F32))
        out_specs.append(pl.BlockSpec((1, tq, LANES), cur))
    kern = functools.partial(_band_kernel, max_back=max_back, kv_heads=kv_heads,
                             use_sinks=sinks is not None, want_lse=want_lse)
    res = pl.pallas_call(kern, grid=grid, in_specs=in_specs, out_specs=out_specs, out_shape=out_shape,
                         compiler_params=_cparams(3), name="band_attn")(*args)
    return tuple(res) if want_lse else res[0]


def _out_kernel(*refs, n_groups, dils):
    ys = refs[:n_groups]
    pos = n_groups
    if n_groups > 1:
        lses = refs[pos:pos + n_groups]
        expand_ref = refs[pos + n_groups]
        pos += n_groups + 1
    z_ref, x_ref, w_ref, g_ref, b_ref, o_ref = refs[pos:pos + 6]
    y_scr, l_scr = refs[pos + 6:pos + 8] if any(d > 1 for d in dils) else (None, None)

    def token_order(ref, scr, d, width):
        if d == 1:
            return ref[...]
        rows = ref.shape[0]
        tiles = width // LANES
        for r in range(d):
            for t in range(tiles):
                scr[t, pl.ds(r, rows, stride=d), :] = ref[:, r * width + t * LANES:r * width + (t + 1) * LANES]
        return jnp.concatenate([scr[t] for t in range(tiles)], axis=1) if tiles > 1 else scr[0]

    if n_groups == 1:
        y = ys[0][...]
    else:
        ls = [token_order(r, l_scr, d, LANES) for r, d in zip(lses, dils)]
        mx = functools.reduce(jnp.maximum, ls)
        es = [jnp.exp(l - mx) for l in ls]
        tot = functools.reduce(lambda a, b: a + b, es)
        y = None
        for e, yr, d in zip(es, ys, dils):
            wts = jnp.dot(e / tot, expand_ref[...], precision=lax.Precision.HIGHEST,
                          preferred_element_type=F32)
            yg = wts * token_order(yr, y_scr, d, D_INNER)
            y = yg if y is None else y + yg
    z = z_ref[...]
    u = (y * (z * jax.nn.sigmoid(z))).astype(BF16)
    r = DN_ALPHA * x_ref[...] + jnp.dot(u, w_ref[...], preferred_element_type=F32)
    mu = jnp.mean(r, axis=1, keepdims=True)
    d = r - mu
    var = jnp.mean(d * d, axis=1, keepdims=True)
    o_ref[...] = d * lax.rsqrt(var + LN_EPS) * g_ref[...] + b_ref[...]


def _out_block(ys, lses, z, x2, w_out_bf, g, b, dils=None, tm=256):
    T, D = x2.shape
    n_groups = len(ys)
    dils = dils or (1,) * n_groups
    row = pl.BlockSpec((tm, D), lambda i: (i, 0))
    in_specs = [pl.BlockSpec((tm // d, d * D_INNER), lambda i: (i, 0)) for d in dils]
    args = list(ys)
    if n_groups > 1:
        in_specs += [pl.BlockSpec((tm // d, d * LANES), lambda i: (i, 0)) for d in dils]
        args += list(lses)
        expand = (np.arange(LANES)[:, None] == (np.arange(D_INNER) // HEAD_DIM)[None, :]).astype(np.float32)
        in_specs.append(pl.BlockSpec((LANES, D_INNER), lambda i: (0, 0)))
        args.append(jnp.asarray(expand))
    in_specs += [row, row, pl.BlockSpec((D_INNER, D), lambda i: (0, 0)),
                 pl.BlockSpec((1, D), lambda i: (0, 0)), pl.BlockSpec((1, D), lambda i: (0, 0))]
    args += [z, x2, w_out_bf, g.reshape(1, D), b.reshape(1, D)]
    scratch = ([pltpu.VMEM((D_INNER // LANES, tm, LANES), F32), pltpu.VMEM((1, tm, LANES), F32)]
               if any(d > 1 for d in dils) else [])
    return pl.pallas_call(
        functools.partial(_out_kernel, n_groups=n_groups, dils=dils), grid=(T // tm,),
        in_specs=in_specs, out_specs=row, out_shape=jax.ShapeDtypeStruct((T, D), F32),
        scratch_shapes=scratch, compiler_params=_cparams(1), name="out_ln")(*args)


def kernel(x, sb_w_in, sb_w_out, ln0_g, ln0_b, moba_w_in, moba_w_out, ln1_g, ln1_b,
           swa_w_in, swa_sinks, swa_w_out, ln2_g, ln2_b, dil_w_in, dil_w_out, ln3_g, ln3_b):
    B, S, D = x.shape
    T = B * S
    x2 = x.reshape(T, D)
    rope = _rope_lane_tables(S)
    W = D_INNER

    segs = ((0, W, False, Q_SCALE_LOG2, 0), (W, W, False, 1.0, 1), (2 * W, W, False, 1.0, 2),
            (3 * W, W, False, 1.0, 3))
    q, k, v, z = _project(x2, sb_w_in.astype(BF16), segs, (W, W, W, W), (BF16, BF16, BF16, F32))
    y = _sb_attention(q.reshape(B, S, W), k.reshape(B, S, W), v.reshape(B, S, W))
    x2 = _out_block([y.reshape(T, W)], None, z, x2, sb_w_out.astype(BF16), ln0_g, ln0_b)

    segs = ((0, W, True, Q_SCALE_LOG2, 0), (W, W, True, 1.0, 1), (2 * W, W, False, 1.0, 2),
            (3 * W, W, False, 1.0, 3))
    q, k, v, z, kmean = _project(x2, moba_w_in.astype(BF16), segs, (W, W, W, W), (BF16, BF16, BF16, F32),
                                 rope_tabs=rope, kmean_seg=1)
    y = _moba_attention(q.reshape(B, S, W), k.reshape(B, S, W), v.reshape(B, S, W),
                        kmean.reshape(B, S // MOBA_BLOCK, W))
    x2 = _out_block([y.reshape(T, W)], None, z, x2, moba_w_out.astype(BF16), ln1_g, ln1_b)

    kvw = SWA_KV_HEADS * HEAD_DIM
    segs = ((0, W, True, Q_SCALE, 0), (W, kvw, True, 1.0, 1), (W + kvw, kvw, False, 1.0, 2),
            (W + 2 * kvw, W, False, 1.0, 3))
    q, k, v, z = _project(x2, swa_w_in.astype(BF16), segs, (W, kvw, kvw, W), (BF16, BF16, BF16, F32),
                          rope_tabs=rope)
    y = _band_attention(q.reshape(B, S, W), k.reshape(B, S, kvw), v.reshape(B, S, kvw), dilation=1,
                        max_back=SWA_WINDOW - 1, kv_heads=SWA_KV_HEADS, sinks=swa_sinks.astype(F32))
    x2 = _out_block([y.reshape(T, W)], None, z, x2, swa_w_out.astype(BF16), ln2_g, ln2_b)

    n_g = len(DILATED_GROUPS)
    segs = []
    for g in range(n_g):
        segs += [((3 * g) * W, W, True, Q_SCALE, 3 * g), ((3 * g + 1) * W, W, True, 1.0, 3 * g + 1),
                 ((3 * g + 2) * W, W, False, 1.0, 3 * g + 2)]
    segs.append((3 * n_g * W, W, False, 1.0, 3 * n_g))
    dils = tuple(d for _, d in DILATED_GROUPS)
    out_dils = tuple(d for d in dils for _ in range(3)) + (1,)
    outs = _project(x2, dil_w_in.astype(BF16), tuple(segs), (W,) * (3 * n_g + 1), (BF16,) * (3 * n_g) + (F32,),
                    rope_tabs=rope, out_dils=out_dils)
    z = outs[-1]
    ys, lses = [], []
    for g, (window, dil) in enumerate(DILATED_GROUPS):
        qg, kg, vg = (outs[3 * g + t].reshape(B, S // dil, dil * W) for t in range(3))
        o, lse = _band_attention(qg, kg, vg, dilation=dil, max_back=window // dil, kv_heads=N_HEADS,
                                 want_lse=True)
        ys.append(o.reshape(T // dil, dil * W))
        lses.append(lse.reshape(T // dil, dil * LANES))
    x2 = _out_block(ys, lses, z, x2, dil_w_out.astype(BF16), ln3_g, ln3_b, dils=dils)
    return x2.reshape(B, S, D)
```

```python
import functools

import jax
import jax.numpy as jnp
import numpy as np
from jax import lax
from jax.experimental import pallas as pl
from jax.experimental.pallas import tpu as pltpu

D_MODEL = 1024
HEAD_DIM = 64
N_HEADS = D_MODEL // HEAD_DIM
D_INNER = N_HEADS * HEAD_DIM
ROPE_THETA = 500000.0
ROT_DIM = HEAD_DIM // 4
LN_EPS = 1e-5
DEPTH = 4
DN_ALPHA = (2.0 * DEPTH) ** 0.25
MOBA_BLOCK = 256
MOBA_TOPK = 3
SWA_WINDOW = 128
SWA_KV_HEADS = 4
DILATED_GROUPS = ((128, 1), (512, 4), (2048, 16))
BAND_BLOCK = 128
ATTN_TILE = 512
SB_SUB = 256
SB_DEAD = -160.0
Q_SCALE = HEAD_DIM ** -0.5
Q_SCALE_LOG2 = Q_SCALE * float(np.log2(np.e))

LANES = 128
PAIRS = D_INNER // LANES
VMEM_LIMIT = 56 * 1024 * 1024

F32 = jnp.float32
BF16 = jnp.bfloat16
NEG_INF = float("-inf")
MASKED = -2.0 ** 60


def _cparams(n_axes):
    return pltpu.CompilerParams(dimension_semantics=("arbitrary",) * n_axes, vmem_limit_bytes=VMEM_LIMIT)


def _nt_dot(a, b):
    return lax.dot_general(a, b, (((1,), (1,)), ((), ())), preferred_element_type=F32)


def _rope_lane_tables(seq_len):
    half = ROT_DIM // 2
    pos = jnp.arange(seq_len, dtype=F32)
    inv = ROPE_THETA ** (-jnp.arange(0, ROT_DIM, 2, dtype=F32) / ROT_DIM)
    ang = pos[:, None] * inv[None, :]
    cos, sin = jnp.cos(ang), jnp.sin(ang)
    hl = np.arange(LANES) % HEAD_DIM
    idx = jnp.asarray(hl % half)
    cos_l, sin_l = cos[:, idx], sin[:, idx]
    c = jnp.where(jnp.asarray(hl < ROT_DIM)[None, :], cos_l, 1.0)
    s1 = jnp.where(jnp.asarray(hl < half)[None, :], -sin_l, 0.0)
    s2 = jnp.where(jnp.asarray((hl >= half) & (hl < ROT_DIM))[None, :], sin_l, 0.0)
    return c.astype(F32), s1.astype(F32), s2.astype(F32)


def _apply_rope(y, c, s1, s2):
    parts = []
    for t in range(y.shape[1] // LANES):
        yt = y[:, t * LANES:(t + 1) * LANES]
        up = pltpu.roll(yt, LANES - ROT_DIM // 2, 1)
        dn = pltpu.roll(yt, ROT_DIM // 2, 1)
        parts.append(yt * c + up * s1 + dn * s2)
    return parts[0] if len(parts) == 1 else jnp.concatenate(parts, axis=1)


def _proj_kernel(*refs, segs, use_rope, kmean_seg, out_dils):
    x_ref, w_ref = refs[0], refs[1]
    pos = 2
    if use_rope:
        c, s1, s2 = refs[2][...], refs[3][...], refs[4][...]
        pos = 5
    streamed = any(d > 1 for d in out_dils)
    outs = refs[pos:-1] if streamed else refs[pos:]
    xb = x_ref[...].astype(BF16)
    for si, (col0, width, rope, scale, oi) in enumerate(segs):
        y = jnp.dot(xb, w_ref[:, col0:col0 + width], preferred_element_type=F32)
        if rope:
            y = _apply_rope(y, c, s1, s2)
        if scale != 1.0:
            y = y * scale
        d = out_dils[oi]
        if d == 1:
            outs[oi][...] = y.astype(outs[oi].dtype)
        else:
            y_scr = refs[-1]
            rows = y.shape[0] // d
            for t in range(width // LANES):
                y_scr[t] = y[:, t * LANES:(t + 1) * LANES]
            for r in range(d):
                for t in range(width // LANES):
                    lanes = slice(r * width + t * LANES, r * width + (t + 1) * LANES)
                    outs[oi][:, lanes] = y_scr[t, pl.ds(r, rows, stride=d), :].astype(outs[oi].dtype)
        if kmean_seg == si:
            tm = y.shape[0]
            km = outs[-1]
            for blk in range(tm // MOBA_BLOCK):
                rows = y[blk * MOBA_BLOCK:(blk + 1) * MOBA_BLOCK, :]
                km[0, blk:blk + 1, :] = jnp.sum(rows, axis=0, keepdims=True) * (1.0 / MOBA_BLOCK)


def _project(x2, w_bf, segs, out_widths, out_dtypes, rope_tabs=None, kmean_seg=None, out_dils=None, tm=256):
    T, D = x2.shape
    N = w_bf.shape[1]
    out_dils = out_dils or (1,) * len(out_widths)
    seq = rope_tabs[0].shape[0] if rope_tabs is not None else None
    in_specs = [pl.BlockSpec((tm, D), lambda i: (i, 0)),
                pl.BlockSpec((D, N), lambda i: (0, 0), pipeline_mode=pl.Buffered(1))]
    args = [x2, w_bf]
    if rope_tabs is not None:
        nblk = seq // tm
        for t in rope_tabs:
            in_specs.append(pl.BlockSpec((tm, LANES), lambda i: (i % nblk, 0)))
            args.append(t)
    out_shape = [jax.ShapeDtypeStruct((T // d, d * w), dt) for w, dt, d in zip(out_widths, out_dtypes, out_dils)]
    out_specs = [pl.BlockSpec((tm // d, d * w), lambda i: (i, 0)) for w, d in zip(out_widths, out_dils)]
    if kmean_seg is not None:
        nb = tm // MOBA_BLOCK
        out_shape.append(jax.ShapeDtypeStruct((T // tm, nb, D_INNER), F32))
        out_specs.append(pl.BlockSpec((1, nb, D_INNER), lambda i: (i, 0, 0)))
    scratch = [pltpu.VMEM((max(out_widths) // LANES, tm, LANES), F32)] if any(d > 1 for d in out_dils) else []
    kern = functools.partial(_proj_kernel, segs=segs, use_rope=rope_tabs is not None, kmean_seg=kmean_seg,
                             out_dils=out_dils)
    return pl.pallas_call(
        kern, grid=(T // tm,), in_specs=in_specs, out_specs=out_specs, out_shape=out_shape,
        scratch_shapes=scratch, compiler_params=_cparams(1), name="in_proj")(*args)


def _sb_kernel(q_ref, k_ref, v_ref, o_ref, g_scr, *, tq):
    i = pl.program_id(2)
    q = q_ref[0]
    lane = lax.broadcasted_iota(jnp.int32, (tq, LANES), 1)
    zero = jnp.zeros_like(q)
    q2 = jnp.concatenate([jnp.where(lane < HEAD_DIM, q, zero), jnp.where(lane >= HEAD_DIM, q, zero)], axis=0)
    sub = SB_SUB
    row = lax.broadcasted_iota(jnp.int32, (sub, sub), 0)
    col = lax.broadcasted_iota(jnp.int32, (sub, sub), 1)
    suffix = jnp.where(row > col, 1.0, 0.0).astype(BF16)

    n_sub = tq // sub

    def weigh(j, slot, diag):
        start = pl.multiple_of(j * tq, tq)
        z = _nt_dot(q2, k_ref[0, pl.ds(start, tq), :])
        neg_abs = lax.bitcast_convert_type(lax.bitcast_convert_type(z, jnp.int32) | jnp.int32(-2 ** 31), F32)
        log_beta = jnp.minimum(z, 0.0) - jnp.log2(1.0 + jnp.exp2(neg_abs))
        l1m = log_beta - z
        if diag:
            qrow = lax.broadcasted_iota(jnp.int32, (2 * tq, tq), 0) & (tq - 1)
            before = lax.broadcasted_iota(jnp.int32, (2 * tq, tq), 1) < qrow
            l1m = jnp.where(before, l1m, 0.0)
            log_beta = jnp.where(before, log_beta, NEG_INF)
        lb = l1m.astype(BF16)
        sums = []
        for s in range(n_sub):
            blk = slice(s * sub, (s + 1) * sub)
            g_scr[slot, :, blk] = log_beta[:, blk] + jnp.dot(lb[:, blk], suffix, preferred_element_type=F32)
            sums.append(jnp.sum(l1m[:, blk], axis=1, keepdims=True))
        return tuple(sums)

    def gather(j, slot, sums, c, acc):
        start = pl.multiple_of(j * tq, tq)
        parts = [None] * n_sub
        for s in reversed(range(n_sub)):
            parts[s] = jnp.exp2(g_scr[slot, :, s * sub:(s + 1) * sub] + c).astype(BF16)
            c = c + sums[s]
        a = jnp.concatenate(parts, axis=1)
        acc = acc + jnp.dot(a, v_ref[0, pl.ds(start, tq), :], preferred_element_type=F32)
        return c, acc

    odd = i % 2
    sums = weigh(i, odd, True)
    carry = (sums, jnp.zeros((2 * tq, 1), F32), jnp.zeros((2 * tq, LANES), F32))

    def single(n, cr):
        c1, a1 = gather(i, 1, cr[0], cr[1], cr[2])
        return weigh(i - 1, 0, False), c1, a1

    def double(n, cr):
        j = i - odd - 2 * n
        c1, a1 = gather(j, 0, cr[0], cr[1], cr[2])
        s1 = weigh(j - 1, 1, False)
        c2, a2 = gather(j - 1, 1, s1, c1, a1)
        return weigh(j - 2, 0, False), c2, a2

    carry = lax.fori_loop(0, odd, single, carry)
    sums, c, acc = lax.fori_loop(0, (i - odd) // 2, double, carry)
    _, acc = gather(0, 0, sums, c, acc)
    o_ref[0] = jnp.where(lane < HEAD_DIM, acc[:tq], acc[tq:])


def _sb_kernel_t(q_ref, k_ref, v_ref, o_ref, g_scr, *, tq):
    i = pl.program_id(2)
    dim = lax.broadcasted_iota(jnp.int32, (LANES, tq), 0)
    q_t = q_ref[0].astype(F32).T
    zero = jnp.zeros_like(q_t)
    q2_t = jnp.concatenate([jnp.where(dim < HEAD_DIM, q_t, zero), jnp.where(dim >= HEAD_DIM, q_t, zero)],
                           axis=1).astype(BF16)
    sub = SB_SUB
    n_sub = tq // sub
    r = lax.broadcasted_iota(jnp.int32, (sub + 8, sub), 0)
    c_ = lax.broadcasted_iota(jnp.int32, (sub + 8, sub), 1)
    sfx = jnp.where((c_ > r) | (r >= sub), 1.0, 0.0).astype(BF16)

    def weigh(j, slot, diag):
        start = pl.multiple_of(j * tq, tq)
        z = jnp.dot(k_ref[0, pl.ds(start, tq), :], q2_t, preferred_element_type=F32)
        neg_abs = lax.bitcast_convert_type(lax.bitcast_convert_type(z, jnp.int32) | jnp.int32(-2 ** 31), F32)
        log_beta = jnp.minimum(z, 0.0) - jnp.log2(1.0 + jnp.exp2(neg_abs))
        l1m = log_beta - z
        if diag:
            before = lax.broadcasted_iota(jnp.int32, (tq, 2 * tq), 0) < (
                lax.broadcasted_iota(jnp.int32, (tq, 2 * tq), 1) & (tq - 1))
            l1m = jnp.where(before, l1m, 0.0)
            log_beta = jnp.where(before, log_beta, NEG_INF)
        lb = l1m.astype(BF16)
        sums = []
        for s in range(n_sub):
            rows = slice(s * sub, (s + 1) * sub)
            ext = jnp.dot(sfx, lb[rows], preferred_element_type=F32)
            g_scr[slot, rows, :] = log_beta[rows] + ext[:sub]
            sums.append(ext[sub:sub + 1])
        return tuple(sums)

    def gather(j, slot, sums, c, acc):
        start = pl.multiple_of(j * tq, tq)
        parts = [None] * n_sub
        for s in reversed(range(n_sub)):
            parts[s] = jnp.exp2(g_scr[slot, s * sub:(s + 1) * sub, :] + c).astype(BF16)
            c = c + sums[s]
        a = jnp.concatenate(parts, axis=0)
        v_t = v_ref[0, pl.ds(start, tq), :].astype(F32).T.astype(BF16)
        acc = acc + jnp.dot(v_t, a, preferred_element_type=F32)
        return c, acc

    odd = i % 2
    sums = weigh(i, odd, True)
    carry = (sums, jnp.zeros((1, 2 * tq), F32), jnp.zeros((LANES, 2 * tq), F32))

    def single(n, cr):
        c1, a1 = gather(i, 1, cr[0], cr[1], cr[2])
        return weigh(i - 1, 0, False), c1, a1

    def double(cr):
        n, _, sums, c, acc = cr
        j = i - odd - 2 * n
        c1, a1 = gather(j, 0, sums, c, acc)
        s1 = weigh(j - 1, 1, False)
        c2, a2 = gather(j - 1, 1, s1, c1, a1)
        return n + 1, (jnp.max(c2) > SB_DEAD).astype(jnp.int32), weigh(j - 2, 0, False), c2, a2

    n_trips = (i - odd) // 2
    sums, c, acc = lax.fori_loop(0, odd, single, carry)
    n, _, sums, c, acc = lax.while_loop(lambda cr: (cr[0] < n_trips) & (cr[1] > 0), double,
                                        (jnp.int32(0), jnp.int32(1), sums, c, acc))
    _, acc = gather(i - odd - 2 * n, 0, sums, c, acc)
    o_ref[0] = jnp.where(dim < HEAD_DIM, acc[:, :tq], acc[:, tq:]).T


def _sb_attention(q, k, v, tq=ATTN_TILE):
    B, S, _ = q.shape
    grid = (B, PAIRS, S // tq)
    return pl.pallas_call(
        functools.partial(_sb_kernel_t, tq=tq), grid=grid,
        in_specs=[pl.BlockSpec((1, tq, LANES), lambda b, p, i: (b, i, p)),
                  pl.BlockSpec((1, S, LANES), lambda b, p, i: (b, 0, p)),
                  pl.BlockSpec((1, S, LANES), lambda b, p, i: (b, 0, p))],
        out_specs=pl.BlockSpec((1, tq, LANES), lambda b, p, i: (b, i, p)),
        out_shape=jax.ShapeDtypeStruct((B, S, D_INNER), F32),
        scratch_shapes=[pltpu.VMEM((2, tq, 2 * tq), F32)],
        compiler_params=_cparams(3), name="sb_attn")(q, k, v)


def _moba_kernel_t(q_ref, k_ref, v_ref, km_ref, o_ref, s_scr, *, nkb, tq):
    i = pl.program_id(2)
    km = km_ref[0]
    per_tile = tq // MOBA_BLOCK
    dim = lax.broadcasted_iota(jnp.int32, (LANES, tq), 0)
    q_t = q_ref[0].astype(F32).T
    zero = jnp.zeros_like(q_t)
    q2_t = jnp.concatenate([jnp.where(dim < HEAD_DIM, q_t, zero), jnp.where(dim >= HEAD_DIM, q_t, zero)],
                           axis=1).astype(BF16)

    km_hi = km.astype(BF16)
    km_lo = (km - km_hi.astype(F32)).astype(BF16)
    gate = jnp.dot(jnp.concatenate([km_hi, km_lo], axis=1), jnp.concatenate([q2_t, q2_t], axis=0),
                   preferred_element_type=F32)
    blk = lax.broadcasted_iota(jnp.int32, (nkb, 2 * tq), 0)
    blk_f = blk.astype(F32)
    qcol = lax.broadcasted_iota(jnp.int32, (nkb, 2 * tq), 1) & (tq - 1)
    q_blk = i * per_tile + qcol // MOBA_BLOCK
    past = blk < q_blk
    g = jnp.where(past, gate, NEG_INF)
    sel = jnp.zeros((nkb, 2 * tq), jnp.bool_)
    for _ in range(MOBA_TOPK):
        mx = jnp.max(g, axis=0, keepdims=True)
        first = jnp.min(jnp.where(g == mx, blk_f, float(nkb)), axis=0, keepdims=True)
        pick = blk_f == first
        sel = sel | (pick & past)
        g = jnp.where(pick, NEG_INF, g)
    bias = jnp.where(sel | (blk == q_blk), 0.0, MASKED).astype(BF16)
    qx_t = jnp.concatenate([q2_t, bias, jnp.zeros((LANES - nkb, 2 * tq), BF16)], axis=0)
    lane_k = lax.broadcasted_iota(jnp.int32, (tq, LANES), 1)
    key_blk = lax.broadcasted_iota(jnp.int32, (tq, LANES), 0) // MOBA_BLOCK

    def score(j, slot, diag):
        st = pl.multiple_of(j * tq, tq)
        onehot = jnp.where(lane_k == j * per_tile + key_blk, 1.0, 0.0).astype(BF16)
        kx = jnp.concatenate([k_ref[0, pl.ds(st, tq), :], onehot], axis=1)
        s = jnp.dot(kx, qx_t, preferred_element_type=F32)
        if diag:
            causal = lax.broadcasted_iota(jnp.int32, (tq, 2 * tq), 0) <= (
                lax.broadcasted_iota(jnp.int32, (tq, 2 * tq), 1) & (tq - 1))
            s = jnp.where(causal, s, NEG_INF)
        s_scr[slot] = s
        return jnp.max(s, axis=0, keepdims=True)

    def absorb(j, slot, m_tile, carry):
        m, l, acc = carry
        st = pl.multiple_of(j * tq, tq)
        v_t = v_ref[0, pl.ds(st, tq), :].astype(F32).T.astype(BF16)
        m_new = jnp.maximum(m, m_tile)
        alpha = jnp.exp2(m - m_new)
        p = jnp.exp2(s_scr[slot] - m_new)
        l = alpha * l + jnp.sum(p, axis=0, keepdims=True)
        acc = alpha * acc + jnp.dot(v_t, p.astype(BF16), preferred_element_type=F32)
        return m_new, l, acc

    odd = i % 2
    m_tile = score(i, odd, True)
    state = (jnp.full((1, 2 * tq), NEG_INF, F32), jnp.zeros((1, 2 * tq), F32), jnp.zeros((LANES, 2 * tq), F32))

    def single(n, cr):
        st1 = absorb(i, 1, cr[0], cr[1])
        return score(i - 1, 0, False), st1

    def double(n, cr):
        j = i - odd - 2 * n
        st1 = absorb(j, 0, cr[0], cr[1])
        m1 = score(j - 1, 1, False)
        st2 = absorb(j - 1, 1, m1, st1)
        return score(j - 2, 0, False), st2

    carry = lax.fori_loop(0, odd, single, (m_tile, state))
    m_tile, state = lax.fori_loop(0, (i - odd) // 2, double, carry)
    _, l, acc = absorb(0, 0, m_tile, state)
    out = acc / l
    o_ref[0] = jnp.where(dim < HEAD_DIM, out[:, :tq], out[:, tq:]).T


def _moba_attention(q, k, v, kmean, tq=ATTN_TILE):
    B, S, _ = q.shape
    nkb = S // MOBA_BLOCK
    grid = (B, PAIRS, S // tq)
    return pl.pallas_call(
        functools.partial(_moba_kernel_t, nkb=nkb, tq=tq), grid=grid,
        in_specs=[pl.BlockSpec((1, tq, LANES), lambda b, p, i: (b, i, p)),
                  pl.BlockSpec((1, S, LANES), lambda b, p, i: (b, 0, p)),
                  pl.BlockSpec((1, S, LANES), lambda b, p, i: (b, 0, p)),
                  pl.BlockSpec((1, nkb, LANES), lambda b, p, i: (b, 0, p))],
        out_specs=pl.BlockSpec((1, tq, LANES), lambda b, p, i: (b, i, p)),
        out_shape=jax.ShapeDtypeStruct((B, S, D_INNER), F32),
        scratch_shapes=[pltpu.VMEM((2, tq, 2 * tq), F32)],
        compiler_params=_cparams(3), name="moba_attn")(q, k, v, kmean)


def _band_kernel(*refs, max_back, kv_heads, use_sinks, want_lse):
    tq = BAND_BLOCK
    pos = 0
    if use_sinks:
        sink_ref = refs[0]
        pos = 1
    q_ref, kp_ref, kc_ref, vp_ref, vc_ref = refs[pos:pos + 5]
    o_ref = refs[pos + 5]
    lse_ref = refs[pos + 6] if want_lse else None
    i = pl.program_id(2)
    rep = N_HEADS // kv_heads

    lane = lax.broadcasted_iota(jnp.int32, (tq, LANES), 1)
    row = lax.broadcasted_iota(jnp.int32, (tq, 2 * tq), 0)
    col = lax.broadcasted_iota(jnp.int32, (tq, 2 * tq), 1)
    dist = row - col + tq
    first_key = jnp.where(i > 0, 0, tq)
    valid = (dist >= 0) & (dist <= max_back) & (col >= first_key)
    low_half = lane < HEAD_DIM

    kcat = jnp.concatenate([kp_ref[0], kc_ref[0]], axis=0)
    vcat = jnp.concatenate([vp_ref[0], vc_ref[0]], axis=0)
    lse_acc = jnp.zeros((tq, LANES), F32)
    for pr in range(PAIRS):
        qf = q_ref[0, :, pr * LANES:(pr + 1) * LANES].astype(F32)
        q_same = qf.astype(BF16)
        q_swap = pltpu.roll(qf, HEAD_DIM, 1).astype(BF16)
        outs = []
        for hh in range(2):
            h = 2 * pr + hh
            g = h // rep
            gh = g % 2
            qsrc = q_same if gh == hh else q_swap
            qm = jnp.where(low_half == (gh == 0), qsrc, jnp.zeros_like(qsrc))
            kt = kcat[:, (g // 2) * LANES:(g // 2 + 1) * LANES]
            vt = vcat[:, (g // 2) * LANES:(g // 2 + 1) * LANES]
            s = jnp.where(valid, _nt_dot(qm, kt), NEG_INF)
            m = jnp.max(s, axis=1, keepdims=True)
            if use_sinks:
                sink = sink_ref[h]
                m = jnp.maximum(m, sink)
            e = jnp.exp(s - m)
            den = jnp.sum(e, axis=1, keepdims=True)
            if use_sinks:
                den = den + jnp.exp(sink - m)
            o = jnp.dot(e.astype(BF16), vt, preferred_element_type=F32) / den
            if gh != hh:
                o = pltpu.roll(o, HEAD_DIM, 1)
            outs.append(o)
            if want_lse:
                lse_acc = jnp.where(lane == h, m + jnp.log(den), lse_acc)
        o_ref[0, :, pr * LANES:(pr + 1) * LANES] = jnp.where(low_half, outs[0], outs[1])
    if want_lse:
        lse_ref[0] = lse_acc


def _band_attention(q, k, v, *, dilation, max_back, kv_heads, sinks=None, want_lse=False):
    B, L, _ = q.shape
    kvw = kv_heads * HEAD_DIM
    tq = BAND_BLOCK
    qv, kv_, vv = q, k, v
    grid = (B, dilation, L // tq)
    cur = lambda b, r, i: (b, i, r)
    prev = lambda b, r, i: (b, jnp.maximum(i - 1, 0), r)
    in_specs = [pl.BlockSpec((1, tq, D_INNER), cur),
                pl.BlockSpec((1, tq, kvw), prev), pl.BlockSpec((1, tq, kvw), cur),
                pl.BlockSpec((1, tq, kvw), prev), pl.BlockSpec((1, tq, kvw), cur)]
    args = [qv, kv_, kv_, vv, vv]
    if sinks is not None:
        in_specs = [pl.BlockSpec(memory_space=pltpu.SMEM)] + in_specs
        args = [sinks] + args
    out_shape = [jax.ShapeDtypeStruct((B, L, dilation * D_INNER), F32)]
    out_specs = [pl.BlockSpec((1, tq, D_INNER), cur)]
    if want_lse:
        out_shape.append(jax.ShapeDtypeStruct((B, L, dilation * LANES), F32))
        out_specs.append(pl.BlockSpec((1, tq, LANES), cur))
    kern = functools.partial(_band_kernel, max_back=max_back, kv_heads=kv_heads,
                             use_sinks=sinks is not None, want_lse=want_lse)
    res = pl.pallas_call(kern, grid=grid, in_specs=in_specs, out_specs=out_specs, out_shape=out_shape,
                         compiler_params=_cparams(3), name="band_attn")(*args)
    return tuple(res) if want_lse else res[0]


def _out_kernel(*refs, n_groups, dils):
    ys = refs[:n_groups]
    pos = n_groups
    if n_groups > 1:
        lses = refs[pos:pos + n_groups]
        expand_ref = refs[pos + n_groups]
        pos += n_groups + 1
    z_ref, x_ref, w_ref, g_ref, b_ref, o_ref = refs[pos:pos + 6]
    y_scr, l_scr = refs[pos + 6:pos + 8] if any(d > 1 for d in dils) else (None, None)

    def token_order(ref, scr, d, width):
        if d == 1:
            return ref[...]
        rows = ref.shape[0]
        tiles = width // LANES
        for r in range(d):
            for t in range(tiles):
                scr[t, pl.ds(r, rows, stride=d), :] = ref[:, r * width + t * LANES:r * width + (t + 1) * LANES]
        return jnp.concatenate([scr[t] for t in range(tiles)], axis=1) if tiles > 1 else scr[0]

    if n_groups == 1:
        y = ys[0][...]
    else:
        ls = [token_order(r, l_scr, d, LANES) for r, d in zip(lses, dils)]
        mx = functools.reduce(jnp.maximum, ls)
        es = [jnp.exp(l - mx) for l in ls]
        tot = functools.reduce(lambda a, b: a + b, es)
        y = None
        for e, yr, d in zip(es, ys, dils):
            wts = jnp.dot(e / tot, expand_ref[...], precision=lax.Precision.HIGHEST,
                          preferred_element_type=F32)
            yg = wts * token_order(yr, y_scr, d, D_INNER)
            y = yg if y is None else y + yg
    z = z_ref[...]
    u = (y * (z * jax.nn.sigmoid(z))).astype(BF16)
    r = DN_ALPHA * x_ref[...] + jnp.dot(u, w_ref[...], preferred_element_type=F32)
    mu = jnp.mean(r, axis=1, keepdims=True)
    d = r - mu
    var = jnp.mean(d * d, axis=1, keepdims=True)
    o_ref[...] = d * lax.rsqrt(var + LN_EPS) * g_ref[...] + b_ref[...]


def _out_block(ys, lses, z, x2, w_out_bf, g, b, dils=None, tm=256):
    T, D = x2.shape
    n_groups = len(ys)
    dils = dils or (1,) * n_groups
    row = pl.BlockSpec((tm, D), lambda i: (i, 0))
    in_specs = [pl.BlockSpec((tm // d, d * D_INNER), lambda i: (i, 0)) for d in dils]
    args = list(ys)
    if n_groups > 1:
        in_specs += [pl.BlockSpec((tm // d, d * LANES), lambda i: (i, 0)) for d in dils]
        args += list(lses)
        expand = (np.arange(LANES)[:, None] == (np.arange(D_INNER) // HEAD_DIM)[None, :]).astype(np.float32)
        in_specs.append(pl.BlockSpec((LANES, D_INNER), lambda i: (0, 0)))
        args.append(jnp.asarray(expand))
    in_specs += [row, row, pl.BlockSpec((D_INNER, D), lambda i: (0, 0)),
                 pl.BlockSpec((1, D), lambda i: (0, 0)), pl.BlockSpec((1, D), lambda i: (0, 0))]
    args += [z, x2, w_out_bf, g.reshape(1, D), b.reshape(1, D)]
    scratch = ([pltpu.VMEM((D_INNER // LANES, tm, LANES), F32), pltpu.VMEM((1, tm, LANES), F32)]
               if any(d > 1 for d in dils) else [])
    return pl.pallas_call(
        functools.partial(_out_kernel, n_groups=n_groups, dils=dils), grid=(T // tm,),
        in_specs=in_specs, out_specs=row, out_shape=jax.ShapeDtypeStruct((T, D), F32),
        scratch_shapes=scratch, compiler_params=_cparams(1), name="out_ln")(*args)


def kernel(x, sb_w_in, sb_w_out, ln0_g, ln0_b, moba_w_in, moba_w_out, ln1_g, ln1_b,
           swa_w_in, swa_sinks, swa_w_out, ln2_g, ln2_b, dil_w_in, dil_w_out, ln3_g, ln3_b):
    B, S, D = x.shape
    T = B * S
    x2 = x.reshape(T, D)
    rope = _rope_lane_tables(S)
    W = D_INNER

    segs = ((0, W, False, Q_SCALE_LOG2, 0), (W, W, False, 1.0, 1), (2 * W, W, False, 1.0, 2),
            (3 * W, W, False, 1.0, 3))
    q, k, v, z = _project(x2, sb_w_in.astype(BF16), segs, (W, W, W, W), (BF16, BF16, BF16, F32))
    y = _sb_attention(q.reshape(B, S, W), k.reshape(B, S, W), v.reshape(B, S, W))
    x2 = _out_block([y.reshape(T, W)], None, z, x2, sb_w_out.astype(BF16), ln0_g, ln0_b)

    segs = ((0, W, True, Q_SCALE_LOG2, 0), (W, W, True, 1.0, 1), (2 * W, W, False, 1.0, 2),
            (3 * W, W, False, 1.0, 3))
    q, k, v, z, kmean = _project(x2, moba_w_in.astype(BF16), segs, (W, W, W, W), (BF16, BF16, BF16, F32),
                                 rope_tabs=rope, kmean_seg=1)
    y = _moba_attention(q.reshape(B, S, W), k.reshape(B, S, W), v.reshape(B, S, W),
                        kmean.reshape(B, S // MOBA_BLOCK, W))
    x2 = _out_block([y.reshape(T, W)], None, z, x2, moba_w_out.astype(BF16), ln1_g, ln1_b)

    kvw = SWA_KV_HEADS * HEAD_DIM
    segs = ((0, W, True, Q_SCALE, 0), (W, kvw, True, 1.0, 1), (W + kvw, kvw, False, 1.0, 2),
            (W + 2 * kvw, W, False, 1.0, 3))
    q, k, v, z = _project(x2, swa_w_in.astype(BF16), segs, (W, kvw, kvw, W), (BF16, BF16, BF16, F32),
                          rope_tabs=rope)
    y = _band_attention(q.reshape(B, S, W), k.reshape(B, S, kvw), v.reshape(B, S, kvw), dilation=1,
                        max_back=SWA_WINDOW - 1, kv_heads=SWA_KV_HEADS, sinks=swa_sinks.astype(F32))
    x2 = _out_block([y.reshape(T, W)], None, z, x2, swa_w_out.astype(BF16), ln2_g, ln2_b)

    n_g = len(DILATED_GROUPS)
    segs = []
    for g in range(n_g):
        segs += [((3 * g) * W, W, True, Q_SCALE, 3 * g), ((3 * g + 1) * W, W, True, 1.0, 3 * g + 1),
                 ((3 * g + 2) * W, W, False, 1.0, 3 * g + 2)]
    segs.append((3 * n_g * W, W, False, 1.0, 3 * n_g))
    dils = tuple(d for _, d in DILATED_GROUPS)
    out_dils = tuple(d for d in dils for _ in range(3)) + (1,)
    outs = _project(x2, dil_w_in.astype(BF16), tuple(segs), (W,) * (3 * n_g + 1), (BF16,) * (3 * n_g) + (F32,),
                    rope_tabs=rope, out_dils=out_dils)
    z = outs[-1]
    ys, lses = [], []
    for g, (window, dil) in enumerate(DILATED_GROUPS):
        qg, kg, vg = (outs[3 * g + t].reshape(B, S // dil, dil * W) for t in range(3))
        o, lse = _band_attention(qg, kg, vg, dilation=dil, max_back=window // dil, kv_heads=N_HEADS,
                                 want_lse=True)
        ys.append(o.reshape(T // dil, dil * W))
        lses.append(lse.reshape(T // dil, dil * LANES))
    x2 = _out_block(ys, lses, z, x2, dil_w_out.astype(BF16), ln3_g, ln3_b, dils=dils)
    return x2.reshape(B, S, D)
```

```python
import functools

import jax
import jax.numpy as jnp
import numpy as np
from jax import lax
from jax.experimental import pallas as pl
from jax.experimental.pallas import tpu as pltpu

D_MODEL = 1024
HEAD_DIM = 64
N_HEADS = D_MODEL // HEAD_DIM
D_INNER = N_HEADS * HEAD_DIM
ROPE_THETA = 500000.0
ROT_DIM = HEAD_DIM // 4
LN_EPS = 1e-5
DEPTH = 4
DN_ALPHA = (2.0 * DEPTH) ** 0.25
MOBA_BLOCK = 256
MOBA_TOPK = 3
SWA_WINDOW = 128
SWA_KV_HEADS = 4
DILATED_GROUPS = ((128, 1), (512, 4), (2048, 16))
BAND_BLOCK = 128
ATTN_TILE = 512
SB_SUB = 256
SB_DEAD = -160.0
Q_SCALE = HEAD_DIM ** -0.5
Q_SCALE_LOG2 = Q_SCALE * float(np.log2(np.e))

LANES = 128
PAIRS = D_INNER // LANES
VMEM_LIMIT = 56 * 1024 * 1024

F32 = jnp.float32
BF16 = jnp.bfloat16
NEG_INF = float("-inf")
MASKED = -2.0 ** 60


def _cparams(n_axes):
    return pltpu.CompilerParams(dimension_semantics=("arbitrary",) * n_axes, vmem_limit_bytes=VMEM_LIMIT)


def _nt_dot(a, b):
    return lax.dot_general(a, b, (((1,), (1,)), ((), ())), preferred_element_type=F32)


def _rope_lane_tables(seq_len):
    half = ROT_DIM // 2
    pos = jnp.arange(seq_len, dtype=F32)
    inv = ROPE_THETA ** (-jnp.arange(0, ROT_DIM, 2, dtype=F32) / ROT_DIM)
    ang = pos[:, None] * inv[None, :]
    cos, sin = jnp.cos(ang), jnp.sin(ang)
    hl = np.arange(LANES) % HEAD_DIM
    idx = jnp.asarray(hl % half)
    cos_l, sin_l = cos[:, idx], sin[:, idx]
    c = jnp.where(jnp.asarray(hl < ROT_DIM)[None, :], cos_l, 1.0)
    s1 = jnp.where(jnp.asarray(hl < half)[None, :], -sin_l, 0.0)
    s2 = jnp.where(jnp.asarray((hl >= half) & (hl < ROT_DIM))[None, :], sin_l, 0.0)
    return c.astype(F32), s1.astype(F32), s2.astype(F32)


def _apply_rope(y, c, s1, s2):
    parts = []
    for t in range(y.shape[1] // LANES):
        yt = y[:, t * LANES:(t + 1) * LANES]
        up = pltpu.roll(yt, LANES - ROT_DIM // 2, 1)
        dn = pltpu.roll(yt, ROT_DIM // 2, 1)
        parts.append(yt * c + up * s1 + dn * s2)
    return parts[0] if len(parts) == 1 else jnp.concatenate(parts, axis=1)


def _proj_kernel(*refs, segs, use_rope, kmean_seg, out_dils):
    x_ref, w_ref = refs[0], refs[1]
    pos = 2
    if use_rope:
        c, s1, s2 = refs[2][...], refs[3][...], refs[4][...]
        pos = 5
    streamed = any(d > 1 for d in out_dils)
    outs = refs[pos:-1] if streamed else refs[pos:]
    xb = x_ref[...].astype(BF16)
    for si, (col0, width, rope, scale, oi) in enumerate(segs):
        y = jnp.dot(xb, w_ref[:, col0:col0 + width], preferred_element_type=F32)
        if rope:
            y = _apply_rope(y, c, s1, s2)
        if scale != 1.0:
            y = y * scale
        d = out_dils[oi]
        if d == 1:
            outs[oi][...] = y.astype(outs[oi].dtype)
        else:
            y_scr = refs[-1]
            rows = y.shape[0] // d
            for t in range(width // LANES):
                y_scr[t] = y[:, t * LANES:(t + 1) * LANES]
            for r in range(d):
                for t in range(width // LANES):
                    lanes = slice(r * width + t * LANES, r * width + (t + 1) * LANES)
                    outs[oi][:, lanes] = y_scr[t, pl.ds(r, rows, stride=d), :].astype(outs[oi].dtype)
        if kmean_seg == si:
            tm = y.shape[0]
            km = outs[-1]
            for blk in range(tm // MOBA_BLOCK):
                rows = y[blk * MOBA_BLOCK:(blk + 1) * MOBA_BLOCK, :]
                km[0, blk:blk + 1, :] = jnp.sum(rows, axis=0, keepdims=True) * (1.0 / MOBA_BLOCK)


def _project(x2, w_bf, segs, out_widths, out_dtypes, rope_tabs=None, kmean_seg=None, out_dils=None, tm=256):
    T, D = x2.shape
    N = w_bf.shape[1]
    out_dils = out_dils or (1,) * len(out_widths)
    seq = rope_tabs[0].shape[0] if rope_tabs is not None else None
    in_specs = [pl.BlockSpec((tm, D), lambda i: (i, 0)),
                pl.BlockSpec((D, N), lambda i: (0, 0), pipeline_mode=pl.Buffered(1))]
    args = [x2, w_bf]
    if rope_tabs is not None:
        nblk = seq // tm
        for t in rope_tabs:
            in_specs.append(pl.BlockSpec((tm, LANES), lambda i: (i % nblk, 0)))
            args.append(t)
    out_shape = [jax.ShapeDtypeStruct((T // d, d * w), dt) for w, dt, d in zip(out_widths, out_dtypes, out_dils)]
    out_specs = [pl.BlockSpec((tm // d, d * w), lambda i: (i, 0)) for w, d in zip(out_widths, out_dils)]
    if kmean_seg is not None:
        nb = tm // MOBA_BLOCK
        out_shape.append(jax.ShapeDtypeStruct((T // tm, nb, D_INNER), F32))
        out_specs.append(pl.BlockSpec((1, nb, D_INNER), lambda i: (i, 0, 0)))
    scratch = [pltpu.VMEM((max(out_widths) // LANES, tm, LANES), F32)] if any(d > 1 for d in out_dils) else []
    kern = functools.partial(_proj_kernel, segs=segs, use_rope=rope_tabs is not None, kmean_seg=kmean_seg,
                             out_dils=out_dils)
    return pl.pallas_call(
        kern, grid=(T // tm,), in_specs=in_specs, out_specs=out_specs, out_shape=out_shape,
        scratch_shapes=scratch, compiler_params=_cparams(1), name="in_proj")(*args)


def _sb_kernel(q_ref, k_ref, v_ref, o_ref, g_scr, *, tq):
    i = pl.program_id(2)
    q = q_ref[0]
    lane = lax.broadcasted_iota(jnp.int32, (tq, LANES), 1)
    zero = jnp.zeros_like(q)
    q2 = jnp.concatenate([jnp.where(lane < HEAD_DIM, q, zero), jnp.where(lane >= HEAD_DIM, q, zero)], axis=0)
    sub = SB_SUB
    row = lax.broadcasted_iota(jnp.int32, (sub, sub), 0)
    col = lax.broadcasted_iota(jnp.int32, (sub, sub), 1)
    suffix = jnp.where(row > col, 1.0, 0.0).astype(BF16)

    n_sub = tq // sub

    def weigh(j, slot, diag):
        start = pl.multiple_of(j * tq, tq)
        z = _nt_dot(q2, k_ref[0, pl.ds(start, tq), :])
        neg_abs = lax.bitcast_convert_type(lax.bitcast_convert_type(z, jnp.int32) | jnp.int32(-2 ** 31), F32)
        log_beta = jnp.minimum(z, 0.0) - jnp.log2(1.0 + jnp.exp2(neg_abs))
        l1m = log_beta - z
        if diag:
            qrow = lax.broadcasted_iota(jnp.int32, (2 * tq, tq), 0) & (tq - 1)
            before = lax.broadcasted_iota(jnp.int32, (2 * tq, tq), 1) < qrow
            l1m = jnp.where(before, l1m, 0.0)
            log_beta = jnp.where(before, log_beta, NEG_INF)
        lb = l1m.astype(BF16)
        sums = []
        for s in range(n_sub):
            blk = slice(s * sub, (s + 1) * sub)
            g_scr[slot, :, blk] = log_beta[:, blk] + jnp.dot(lb[:, blk], suffix, preferred_element_type=F32)
            sums.append(jnp.sum(l1m[:, blk], axis=1, keepdims=True))
        return tuple(sums)

    def gather(j, slot, sums, c, acc):
        start = pl.multiple_of(j * tq, tq)
        parts = [None] * n_sub
        for s in reversed(range(n_sub)):
            parts[s] = jnp.exp2(g_scr[slot, :, s * sub:(s + 1) * sub] + c).astype(BF16)
            c = c + sums[s]
        a = jnp.concatenate(parts, axis=1)
        acc = acc + jnp.dot(a, v_ref[0, pl.ds(start, tq), :], preferred_element_type=F32)
        return c, acc

    odd = i % 2
    sums = weigh(i, odd, True)
    carry = (sums, jnp.zeros((2 * tq, 1), F32), jnp.zeros((2 * tq, LANES), F32))

    def single(n, cr):
        c1, a1 = gather(i, 1, cr[0], cr[1], cr[2])
        return weigh(i - 1, 0, False), c1, a1

    def double(n, cr):
        j = i - odd - 2 * n
        c1, a1 = gather(j, 0, cr[0], cr[1], cr[2])
        s1 = weigh(j - 1, 1, False)
        c2, a2 = gather(j - 1, 1, s1, c1, a1)
        return weigh(j - 2, 0, False), c2, a2

    carry = lax.fori_loop(0, odd, single, carry)
    sums, c, acc = lax.fori_loop(0, (i - odd) // 2, double, carry)
    _, acc = gather(0, 0, sums, c, acc)
    o_ref[0] = jnp.where(lane < HEAD_DIM, acc[:tq], acc[tq:])


def _sb_kernel_t(q_ref, k_ref, v_ref, o_ref, g_scr, *, tq):
    i = pl.program_id(2)
    dim = lax.broadcasted_iota(jnp.int32, (LANES, tq), 0)
    q_t = q_ref[0].astype(F32).T
    zero = jnp.zeros_like(q_t)
    q2_t = jnp.concatenate([jnp.where(dim < HEAD_DIM, q_t, zero), jnp.where(dim >= HEAD_DIM, q_t, zero)],
                           axis=1).astype(BF16)
    sub = SB_SUB
    n_sub = tq // sub
    r = lax.broadcasted_iota(jnp.int32, (sub + 8, sub), 0)
    c_ = lax.broadcasted_iota(jnp.int32, (sub + 8, sub), 1)
    sfx = jnp.where((c_ > r) | (r >= sub), 1.0, 0.0).astype(BF16)

    def weigh(j, slot, diag):
        start = pl.multiple_of(j * tq, tq)
        z = jnp.dot(k_ref[0, pl.ds(start, tq), :], q2_t, preferred_element_type=F32)
        neg_abs = lax.bitcast_convert_type(lax.bitcast_convert_type(z, jnp.int32) | jnp.int32(-2 ** 31), F32)
        log_beta = jnp.minimum(z, 0.0) - jnp.log2(1.0 + jnp.exp2(neg_abs))
        l1m = log_beta - z
        if diag:
            before = lax.broadcasted_iota(jnp.int32, (tq, 2 * tq), 0) < (
                lax.broadcasted_iota(jnp.int32, (tq, 2 * tq), 1) & (tq - 1))
            l1m = jnp.where(before, l1m, 0.0)
            log_beta = jnp.where(before, log_beta, NEG_INF)
        lb = l1m.astype(BF16)
        sums = []
        for s in range(n_sub):
            rows = slice(s * sub, (s + 1) * sub)
            ext = jnp.dot(sfx, lb[rows], preferred_element_type=F32)
            g_scr[slot, rows, :] = log_beta[rows] + ext[:sub]
            sums.append(ext[sub:sub + 1])
        return tuple(sums)

    def gather(j, slot, sums, c, acc):
        start = pl.multiple_of(j * tq, tq)
        parts = [None] * n_sub
        for s in reversed(range(n_sub)):
            parts[s] = jnp.exp2(g_scr[slot, s * sub:(s + 1) * sub, :] + c).astype(BF16)
            c = c + sums[s]
        a = jnp.concatenate(parts, axis=0)
        v_t = v_ref[0, pl.ds(start, tq), :].astype(F32).T.astype(BF16)
        acc = acc + jnp.dot(v_t, a, preferred_element_type=F32)
        return c, acc

    odd = i % 2
    sums = weigh(i, odd, True)
    carry = (sums, jnp.zeros((1, 2 * tq), F32), jnp.zeros((LANES, 2 * tq), F32))

    def single(n, cr):
        c1, a1 = gather(i, 1, cr[0], cr[1], cr[2])
        return weigh(i - 1, 0, False), c1, a1

    def alive(c):
        return (jnp.max(c) > SB_DEAD).astype(jnp.int32)

    def double(cr):
        n, _, sums, c, acc = cr
        j = i - odd - 2 * n
        c1, a1 = gather(j, 0, sums, c, acc)
        s1 = weigh(j - 1, 1, False)

        def rest(_):
            c2, a2 = gather(j - 1, 1, s1, c1, a1)
            return alive(c2), weigh(j - 2, 0, False), c2, a2

        live, s2, c2, a2 = lax.cond(alive(c1) > 0, rest, lambda _: (jnp.int32(0), s1, c1, a1), None)
        return n + 1, live, s2, c2, a2

    n_trips = (i - odd) // 2
    sums, c, acc = lax.fori_loop(0, odd, single, carry)
    n, live, sums, c, acc = lax.while_loop(lambda cr: (cr[0] < n_trips) & (cr[1] > 0), double,
                                           (jnp.int32(0), alive(c), sums, c, acc))
    acc = lax.cond(live > 0, lambda _: gather(i - odd - 2 * n, 0, sums, c, acc)[1], lambda _: acc, None)
    o_ref[0] = jnp.where(dim < HEAD_DIM, acc[:, :tq], acc[:, tq:]).T


def _sb_attention(q, k, v, tq=ATTN_TILE):
    B, S, _ = q.shape
    grid = (B, PAIRS, S // tq)
    return pl.pallas_call(
        functools.partial(_sb_kernel_t, tq=tq), grid=grid,
        in_specs=[pl.BlockSpec((1, tq, LANES), lambda b, p, i: (b, i, p)),
                  pl.BlockSpec((1, S, LANES), lambda b, p, i: (b, 0, p)),
                  pl.BlockSpec((1, S, LANES), lambda b, p, i: (b, 0, p))],
        out_specs=pl.BlockSpec((1, tq, LANES), lambda b, p, i: (b, i, p)),
        out_shape=jax.ShapeDtypeStruct((B, S, D_INNER), F32),
        scratch_shapes=[pltpu.VMEM((2, tq, 2 * tq), F32)],
        compiler_params=_cparams(3), name="sb_attn")(q, k, v)


def _moba_kernel_t(q_ref, k_ref, v_ref, km_ref, o_ref, s_scr, *, nkb, tq):
    i = pl.program_id(2)
    km = km_ref[0]
    per_tile = tq // MOBA_BLOCK
    dim = lax.broadcasted_iota(jnp.int32, (LANES, tq), 0)
    q_t = q_ref[0].astype(F32).T
    zero = jnp.zeros_like(q_t)
    q2_t = jnp.concatenate([jnp.where(dim < HEAD_DIM, q_t, zero), jnp.where(dim >= HEAD_DIM, q_t, zero)],
                           axis=1).astype(BF16)

    km_hi = km.astype(BF16)
    km_lo = (km - km_hi.astype(F32)).astype(BF16)
    gate = jnp.dot(jnp.concatenate([km_hi, km_lo], axis=1), jnp.concatenate([q2_t, q2_t], axis=0),
                   preferred_element_type=F32)
    blk = lax.broadcasted_iota(jnp.int32, (nkb, 2 * tq), 0)
    blk_f = blk.astype(F32)
    qcol = lax.broadcasted_iota(jnp.int32, (nkb, 2 * tq), 1) & (tq - 1)
    q_blk = i * per_tile + qcol // MOBA_BLOCK
    past = blk < q_blk
    g = jnp.where(past, gate, NEG_INF)
    sel = jnp.zeros((nkb, 2 * tq), jnp.bool_)
    for _ in range(MOBA_TOPK):
        mx = jnp.max(g, axis=0, keepdims=True)
        first = jnp.min(jnp.where(g == mx, blk_f, float(nkb)), axis=0, keepdims=True)
        pick = blk_f == first
        sel = sel | (pick & past)
        g = jnp.where(pick, NEG_INF, g)
    bias = jnp.where(sel | (blk == q_blk), 0.0, MASKED).astype(BF16)
    qx_t = jnp.concatenate([q2_t, bias, jnp.zeros((LANES - nkb, 2 * tq), BF16)], axis=0)
    lane_k = lax.broadcasted_iota(jnp.int32, (tq, LANES), 1)
    key_blk = lax.broadcasted_iota(jnp.int32, (tq, LANES), 0) // MOBA_BLOCK

    def score(j, slot, diag):
        st = pl.multiple_of(j * tq, tq)
        onehot = jnp.where(lane_k == j * per_tile + key_blk, 1.0, 0.0).astype(BF16)
        kx = jnp.concatenate([k_ref[0, pl.ds(st, tq), :], onehot], axis=1)
        s = jnp.dot(kx, qx_t, preferred_element_type=F32)
        if diag:
            causal = lax.broadcasted_iota(jnp.int32, (tq, 2 * tq), 0) <= (
                lax.broadcasted_iota(jnp.int32, (tq, 2 * tq), 1) & (tq - 1))
            s = jnp.where(causal, s, NEG_INF)
        s_scr[slot] = s
        return jnp.max(s, axis=0, keepdims=True)

    def absorb(j, slot, m_tile, carry):
        m, l, acc = carry
        st = pl.multiple_of(j * tq, tq)
        v_t = v_ref[0, pl.ds(st, tq), :].astype(F32).T.astype(BF16)
        m_new = jnp.maximum(m, m_tile)
        alpha = jnp.exp2(m - m_new)
        p = jnp.exp2(s_scr[slot] - m_new)
        l = alpha * l + jnp.sum(p, axis=0, keepdims=True)
        acc = alpha * acc + jnp.dot(v_t, p.astype(BF16), preferred_element_type=F32)
        return m_new, l, acc

    odd = i % 2
    m_tile = score(i, odd, True)
    state = (jnp.full((1, 2 * tq), NEG_INF, F32), jnp.zeros((1, 2 * tq), F32), jnp.zeros((LANES, 2 * tq), F32))

    def single(n, cr):
        st1 = absorb(i, 1, cr[0], cr[1])
        return score(i - 1, 0, False), st1

    def double(n, cr):
        j = i - odd - 2 * n
        st1 = absorb(j, 0, cr[0], cr[1])
        m1 = score(j - 1, 1, False)
        st2 = absorb(j - 1, 1, m1, st1)
        return score(j - 2, 0, False), st2

    carry = lax.fori_loop(0, odd, single, (m_tile, state))
    m_tile, state = lax.fori_loop(0, (i - odd) // 2, double, carry)
    _, l, acc = absorb(0, 0, m_tile, state)
    out = acc / l
    o_ref[0] = jnp.where(dim < HEAD_DIM, out[:, :tq], out[:, tq:]).T


def _moba_attention(q, k, v, kmean, tq=ATTN_TILE):
    B, S, _ = q.shape
    nkb = S // MOBA_BLOCK
    grid = (B, PAIRS, S // tq)
    return pl.pallas_call(
        functools.partial(_moba_kernel_t, nkb=nkb, tq=tq), grid=grid,
        in_specs=[pl.BlockSpec((1, tq, LANES), lambda b, p, i: (b, i, p)),
                  pl.BlockSpec((1, S, LANES), lambda b, p, i: (b, 0, p)),
                  pl.BlockSpec((1, S, LANES), lambda b, p, i: (b, 0, p)),
                  pl.BlockSpec((1, nkb, LANES), lambda b, p, i: (b, 0, p))],
        out_specs=pl.BlockSpec((1, tq, LANES), lambda b, p, i: (b, i, p)),
        out_shape=jax.ShapeDtypeStruct((B, S, D_INNER), F32),
        scratch_shapes=[pltpu.VMEM((2, tq, 2 * tq), F32)],
        compiler_params=_cparams(3), name="moba_attn")(q, k, v, kmean)


def _band_kernel(*refs, max_back, kv_heads, use_sinks, want_lse):
    tq = BAND_BLOCK
    pos = 0
    if use_sinks:
        sink_ref = refs[0]
        pos = 1
    q_ref, kp_ref, kc_ref, vp_ref, vc_ref = refs[pos:pos + 5]
    o_ref = refs[pos + 5]
    lse_ref = refs[pos + 6] if want_lse else None
    i = pl.program_id(2)
    rep = N_HEADS // kv_heads

    lane = lax.broadcasted_iota(jnp.int32, (tq, LANES), 1)
    row = lax.broadcasted_iota(jnp.int32, (tq, 2 * tq), 0)
    col = lax.broadcasted_iota(jnp.int32, (tq, 2 * tq), 1)
    dist = row - col + tq
    first_key = jnp.where(i > 0, 0, tq)
    valid = (dist >= 0) & (dist <= max_back) & (col >= first_key)
    low_half = lane < HEAD_DIM

    kcat = jnp.concatenate([kp_ref[0], kc_ref[0]], axis=0)
    vcat = jnp.concatenate([vp_ref[0], vc_ref[0]], axis=0)
    lse_acc = jnp.zeros((tq, LANES), F32)
    for pr in range(PAIRS):
        qf = q_ref[0, :, pr * LANES:(pr + 1) * LANES].astype(F32)
        q_same = qf.astype(BF16)
        q_swap = pltpu.roll(qf, HEAD_DIM, 1).astype(BF16)
        outs = []
        for hh in range(2):
            h = 2 * pr + hh
            g = h // rep
            gh = g % 2
            qsrc = q_same if gh == hh else q_swap
            qm = jnp.where(low_half == (gh == 0), qsrc, jnp.zeros_like(qsrc))
            kt = kcat[:, (g // 2) * LANES:(g // 2 + 1) * LANES]
            vt = vcat[:, (g // 2) * LANES:(g // 2 + 1) * LANES]
            s = jnp.where(valid, _nt_dot(qm, kt), NEG_INF)
            m = jnp.max(s, axis=1, keepdims=True)
            if use_sinks:
                sink = sink_ref[h]
                m = jnp.maximum(m, sink)
            e = jnp.exp(s - m)
            den = jnp.sum(e, axis=1, keepdims=True)
            if use_sinks:
                den = den + jnp.exp(sink - m)
            o = jnp.dot(e.astype(BF16), vt, preferred_element_type=F32) / den
            if gh != hh:
                o = pltpu.roll(o, HEAD_DIM, 1)
            outs.append(o)
            if want_lse:
                lse_acc = jnp.where(lane == h, m + jnp.log(den), lse_acc)
        o_ref[0, :, pr * LANES:(pr + 1) * LANES] = jnp.where(low_half, outs[0], outs[1])
    if want_lse:
        lse_ref[0] = lse_acc


def _band_attention(q, k, v, *, dilation, max_back, kv_heads, sinks=None, want_lse=False):
    B, L, _ = q.shape
    kvw = kv_heads * HEAD_DIM
    tq = BAND_BLOCK
    qv, kv_, vv = q, k, v
    grid = (B, dilation, L // tq)
    cur = lambda b, r, i: (b, i, r)
    prev = lambda b, r, i: (b, jnp.maximum(i - 1, 0), r)
    in_specs = [pl.BlockSpec((1, tq, D_INNER), cur),
                pl.BlockSpec((1, tq, kvw), prev), pl.BlockSpec((1, tq, kvw), cur),
                pl.BlockSpec((1, tq, kvw), prev), pl.BlockSpec((1, tq, kvw), cur)]
    args = [qv, kv_, kv_, vv, vv]
    if sinks is not None:
        in_specs = [pl.BlockSpec(memory_space=pltpu.SMEM)] + in_specs
        args = [sinks] + args
    out_shape = [jax.ShapeDtypeStruct((B, L, dilation * D_INNER), F32)]
    out_specs = [pl.BlockSpec((1, tq, D_INNER), cur)]
    if want_lse:
        out_shape.append(jax.ShapeDtypeStruct((B, L, dilation * LANES), F32))
        out_specs.append(pl.BlockSpec((1, tq, LANES), cur))
    kern = functools.partial(_band_kernel, max_back=max_back, kv_heads=kv_heads,
                             use_sinks=sinks is not None, want_lse=want_lse)
    res = pl.pallas_call(kern, grid=grid, in_specs=in_specs, out_specs=out_specs, out_shape=out_shape,
                         compiler_params=_cparams(3), name="band_attn")(*args)
    return tuple(res) if want_lse else res[0]


def _out_kernel(*refs, n_groups, dils):
    ys = refs[:n_groups]
    pos = n_groups
    if n_groups > 1:
        lses = refs[pos:pos + n_groups]
        expand_ref = refs[pos + n_groups]
        pos += n_groups + 1
    z_ref, x_ref, w_ref, g_ref, b_ref, o_ref = refs[pos:pos + 6]
    y_scr, l_scr = refs[pos + 6:pos + 8] if any(d > 1 for d in dils) else (None, None)

    def token_order(ref, scr, d, width):
        if d == 1:
            return ref[...]
        rows = ref.shape[0]
        tiles = width // LANES
        for r in range(d):
            for t in range(tiles):
                scr[t, pl.ds(r, rows, stride=d), :] = ref[:, r * width + t * LANES:r * width + (t + 1) * LANES]
        return jnp.concatenate([scr[t] for t in range(tiles)], axis=1) if tiles > 1 else scr[0]

    if n_groups == 1:
        y = ys[0][...]
    else:
        ls = [token_order(r, l_scr, d, LANES) for r, d in zip(lses, dils)]
        mx = functools.reduce(jnp.maximum, ls)
        es = [jnp.exp(l - mx) for l in ls]
        tot = functools.reduce(lambda a, b: a + b, es)
        y = None
        for e, yr, d in zip(es, ys, dils):
            wts = jnp.dot(e / tot, expand_ref[...], precision=lax.Precision.HIGHEST,
                          preferred_element_type=F32)
            yg = wts * token_order(yr, y_scr, d, D_INNER)
            y = yg if y is None else y + yg
    z = z_ref[...]
    u = (y * (z * jax.nn.sigmoid(z))).astype(BF16)
    r = DN_ALPHA * x_ref[...] + jnp.dot(u, w_ref[...], preferred_element_type=F32)
    mu = jnp.mean(r, axis=1, keepdims=True)
    d = r - mu
    var = jnp.mean(d * d, axis=1, keepdims=True)
    o_ref[...] = d * lax.rsqrt(var + LN_EPS) * g_ref[...] + b_ref[...]


def _out_block(ys, lses, z, x2, w_out_bf, g, b, dils=None, tm=256):
    T, D = x2.shape
    n_groups = len(ys)
    dils = dils or (1,) * n_groups
    row = pl.BlockSpec((tm, D), lambda i: (i, 0))
    in_specs = [pl.BlockSpec((tm // d, d * D_INNER), lambda i: (i, 0)) for d in dils]
    args = list(ys)
    if n_groups > 1:
        in_specs += [pl.BlockSpec((tm // d, d * LANES), lambda i: (i, 0)) for d in dils]
        args += list(lses)
        expand = (np.arange(LANES)[:, None] == (np.arange(D_INNER) // HEAD_DIM)[None, :]).astype(np.float32)
        in_specs.append(pl.BlockSpec((LANES, D_INNER), lambda i: (0, 0)))
        args.append(jnp.asarray(expand))
    in_specs += [row, row, pl.BlockSpec((D_INNER, D), lambda i: (0, 0)),
                 pl.BlockSpec((1, D), lambda i: (0, 0)), pl.BlockSpec((1, D), lambda i: (0, 0))]
    args += [z, x2, w_out_bf, g.reshape(1, D), b.reshape(1, D)]
    scratch = ([pltpu.VMEM((D_INNER // LANES, tm, LANES), F32), pltpu.VMEM((1, tm, LANES), F32)]
               if any(d > 1 for d in dils) else [])
    return pl.pallas_call(
        functools.partial(_out_kernel, n_groups=n_groups, dils=dils), grid=(T // tm,),
        in_specs=in_specs, out_specs=row, out_shape=jax.ShapeDtypeStruct((T, D), F32),
        scratch_shapes=scratch, compiler_params=_cparams(1), name="out_ln")(*args)


def kernel(x, sb_w_in, sb_w_out, ln0_g, ln0_b, moba_w_in, moba_w_out, ln1_g, ln1_b,
           swa_w_in, swa_sinks, swa_w_out, ln2_g, ln2_b, dil_w_in, dil_w_out, ln3_g, ln3_b):
    B, S, D = x.shape
    T = B * S
    x2 = x.reshape(T, D)
    rope = _rope_lane_tables(S)
    W = D_INNER

    segs = ((0, W, False, Q_SCALE_LOG2, 0), (W, W, False, 1.0, 1), (2 * W, W, False, 1.0, 2),
            (3 * W, W, False, 1.0, 3))
    q, k, v, z = _project(x2, sb_w_in.astype(BF16), segs, (W, W, W, W), (BF16, BF16, BF16, F32))
    y = _sb_attention(q.reshape(B, S, W), k.reshape(B, S, W), v.reshape(B, S, W))
    x2 = _out_block([y.reshape(T, W)], None, z, x2, sb_w_out.astype(BF16), ln0_g, ln0_b)

    segs = ((0, W, True, Q_SCALE_LOG2, 0), (W, W, True, 1.0, 1), (2 * W, W, False, 1.0, 2),
            (3 * W, W, False, 1.0, 3))
    q, k, v, z, kmean = _project(x2, moba_w_in.astype(BF16), segs, (W, W, W, W), (BF16, BF16, BF16, F32),
                                 rope_tabs=rope, kmean_seg=1)
    y = _moba_attention(q.reshape(B, S, W), k.reshape(B, S, W), v.reshape(B, S, W),
                        kmean.reshape(B, S // MOBA_BLOCK, W))
    x2 = _out_block([y.reshape(T, W)], None, z, x2, moba_w_out.astype(BF16), ln1_g, ln1_b)

    kvw = SWA_KV_HEADS * HEAD_DIM
    segs = ((0, W, True, Q_SCALE, 0), (W, kvw, True, 1.0, 1), (W + kvw, kvw, False, 1.0, 2),
            (W + 2 * kvw, W, False, 1.0, 3))
    q, k, v, z = _project(x2, swa_w_in.astype(BF16), segs, (W, kvw, kvw, W), (BF16, BF16, BF16, F32),
                          rope_tabs=rope)
    y = _band_attention(q.reshape(B, S, W), k.reshape(B, S, kvw), v.reshape(B, S, kvw), dilation=1,
                        max_back=SWA_WINDOW - 1, kv_heads=SWA_KV_HEADS, sinks=swa_sinks.astype(F32))
    x2 = _out_block([y.reshape(T, W)], None, z, x2, swa_w_out.astype(BF16), ln2_g, ln2_b)

    n_g = len(DILATED_GROUPS)
    segs = []
    for g in range(n_g):
        segs += [((3 * g) * W, W, True, Q_SCALE, 3 * g), ((3 * g + 1) * W, W, True, 1.0, 3 * g + 1),
                 ((3 * g + 2) * W, W, False, 1.0, 3 * g + 2)]
    segs.append((3 * n_g * W, W, False, 1.0, 3 * n_g))
    dils = tuple(d for _, d in DILATED_GROUPS)
    out_dils = tuple(d for d in dils for _ in range(3)) + (1,)
    outs = _project(x2, dil_w_in.astype(BF16), tuple(segs), (W,) * (3 * n_g + 1), (BF16,) * (3 * n_g) + (F32,),
                    rope_tabs=rope, out_dils=out_dils)
    z = outs[-1]
    ys, lses = [], []
    for g, (window, dil) in enumerate(DILATED_GROUPS):
        qg, kg, vg = (outs[3 * g + t].reshape(B, S // dil, dil * W) for t in range(3))
        o, lse = _band_attention(qg, kg, vg, dilation=dil, max_back=window // dil, kv_heads=N_HEADS,
                                 want_lse=True)
        ys.append(o.reshape(T // dil, dil * W))
        lses.append(lse.reshape(T // dil, dil * LANES))
    x2 = _out_block(ys, lses, z, x2, dil_w_out.astype(BF16), ln3_g, ln3_b, dils=dils)
    return x2.reshape(B, S, D)
```

```python
import functools

import jax
import jax.numpy as jnp
import numpy as np
from jax import lax
from jax.experimental import pallas as pl
from jax.experimental.pallas import tpu as pltpu

D_MODEL = 1024
HEAD_DIM = 64
N_HEADS = D_MODEL // HEAD_DIM
D_INNER = N_HEADS * HEAD_DIM
ROPE_THETA = 500000.0
ROT_DIM = HEAD_DIM // 4
LN_EPS = 1e-5
DEPTH = 4
DN_ALPHA = (2.0 * DEPTH) ** 0.25
MOBA_BLOCK = 256
MOBA_TOPK = 3
SWA_WINDOW = 128
SWA_KV_HEADS = 4
DILATED_GROUPS = ((128, 1), (512, 4), (2048, 16))
BAND_BLOCK = 128
ATTN_TILE = 512
PROJ_ROWS_WIDE = 512
SB_SUB = 256
SB_DEAD = -160.0
Q_SCALE = HEAD_DIM ** -0.5
Q_SCALE_LOG2 = Q_SCALE * float(np.log2(np.e))

LANES = 128
PAIRS = D_INNER // LANES
VMEM_LIMIT = 56 * 1024 * 1024

F32 = jnp.float32
BF16 = jnp.bfloat16
NEG_INF = float("-inf")
MASKED = -2.0 ** 60


def _cparams(n_axes):
    return pltpu.CompilerParams(dimension_semantics=("arbitrary",) * n_axes, vmem_limit_bytes=VMEM_LIMIT)


def _nt_dot(a, b):
    return lax.dot_general(a, b, (((1,), (1,)), ((), ())), preferred_element_type=F32)


def _rope_lane_tables(seq_len):
    half = ROT_DIM // 2
    pos = jnp.arange(seq_len, dtype=F32)
    inv = ROPE_THETA ** (-jnp.arange(0, ROT_DIM, 2, dtype=F32) / ROT_DIM)
    ang = pos[:, None] * inv[None, :]
    cos, sin = jnp.cos(ang), jnp.sin(ang)
    hl = np.arange(LANES) % HEAD_DIM
    idx = jnp.asarray(hl % half)
    cos_l, sin_l = cos[:, idx], sin[:, idx]
    c = jnp.where(jnp.asarray(hl < ROT_DIM)[None, :], cos_l, 1.0)
    s1 = jnp.where(jnp.asarray(hl < half)[None, :], -sin_l, 0.0)
    s2 = jnp.where(jnp.asarray((hl >= half) & (hl < ROT_DIM))[None, :], sin_l, 0.0)
    return c.astype(F32), s1.astype(F32), s2.astype(F32)


def _apply_rope(y, c, s1, s2):
    parts = []
    for t in range(y.shape[1] // LANES):
        yt = y[:, t * LANES:(t + 1) * LANES]
        up = pltpu.roll(yt, LANES - ROT_DIM // 2, 1)
        dn = pltpu.roll(yt, ROT_DIM // 2, 1)
        parts.append(yt * c + up * s1 + dn * s2)
    return parts[0] if len(parts) == 1 else jnp.concatenate(parts, axis=1)


def _proj_kernel(*refs, segs, use_rope, kmean_seg, out_dils):
    x_ref, w_ref = refs[0], refs[1]
    pos = 2
    if use_rope:
        c, s1, s2 = refs[2][...], refs[3][...], refs[4][...]
        pos = 5
    streamed = any(d > 1 for d in out_dils)
    outs = refs[pos:-1] if streamed else refs[pos:]
    xb = x_ref[...].astype(BF16)
    for si, (col0, width, rope, scale, oi) in enumerate(segs):
        y = jnp.dot(xb, w_ref[:, col0:col0 + width], preferred_element_type=F32)
        if rope:
            y = _apply_rope(y, c, s1, s2)
        if scale != 1.0:
            y = y * scale
        d = out_dils[oi]
        if d == 1:
            outs[oi][...] = y.astype(outs[oi].dtype)
        else:
            y_scr = refs[-1]
            rows = y.shape[0] // d
            for t in range(width // LANES):
                y_scr[t] = y[:, t * LANES:(t + 1) * LANES]
            for r in range(d):
                for t in range(width // LANES):
                    lanes = slice(r * width + t * LANES, r * width + (t + 1) * LANES)
                    outs[oi][:, lanes] = y_scr[t, pl.ds(r, rows, stride=d), :].astype(outs[oi].dtype)
        if kmean_seg == si:
            tm = y.shape[0]
            km = outs[-1]
            for blk in range(tm // MOBA_BLOCK):
                rows = y[blk * MOBA_BLOCK:(blk + 1) * MOBA_BLOCK, :]
                km[0, blk:blk + 1, :] = jnp.sum(rows, axis=0, keepdims=True) * (1.0 / MOBA_BLOCK)


def _project(x2, w_bf, segs, out_widths, out_dtypes, rope_tabs=None, kmean_seg=None, out_dils=None, tm=256):
    T, D = x2.shape
    N = w_bf.shape[1]
    out_dils = out_dils or (1,) * len(out_widths)
    seq = rope_tabs[0].shape[0] if rope_tabs is not None else None
    in_specs = [pl.BlockSpec((tm, D), lambda i: (i, 0)),
                pl.BlockSpec((D, N), lambda i: (0, 0), pipeline_mode=pl.Buffered(1))]
    args = [x2, w_bf]
    if rope_tabs is not None:
        nblk = seq // tm
        for t in rope_tabs:
            in_specs.append(pl.BlockSpec((tm, LANES), lambda i: (i % nblk, 0)))
            args.append(t)
    out_shape = [jax.ShapeDtypeStruct((T // d, d * w), dt) for w, dt, d in zip(out_widths, out_dtypes, out_dils)]
    out_specs = [pl.BlockSpec((tm // d, d * w), lambda i: (i, 0)) for w, d in zip(out_widths, out_dils)]
    if kmean_seg is not None:
        nb = tm // MOBA_BLOCK
        out_shape.append(jax.ShapeDtypeStruct((T // tm, nb, D_INNER), F32))
        out_specs.append(pl.BlockSpec((1, nb, D_INNER), lambda i: (i, 0, 0)))
    scratch = [pltpu.VMEM((max(out_widths) // LANES, tm, LANES), F32)] if any(d > 1 for d in out_dils) else []
    kern = functools.partial(_proj_kernel, segs=segs, use_rope=rope_tabs is not None, kmean_seg=kmean_seg,
                             out_dils=out_dils)
    return pl.pallas_call(
        kern, grid=(T // tm,), in_specs=in_specs, out_specs=out_specs, out_shape=out_shape,
        scratch_shapes=scratch, compiler_params=_cparams(1), name="in_proj")(*args)


def _sb_kernel(q_ref, k_ref, v_ref, o_ref, g_scr, *, tq):
    i = pl.program_id(2)
    q = q_ref[0]
    lane = lax.broadcasted_iota(jnp.int32, (tq, LANES), 1)
    zero = jnp.zeros_like(q)
    q2 = jnp.concatenate([jnp.where(lane < HEAD_DIM, q, zero), jnp.where(lane >= HEAD_DIM, q, zero)], axis=0)
    sub = SB_SUB
    row = lax.broadcasted_iota(jnp.int32, (sub, sub), 0)
    col = lax.broadcasted_iota(jnp.int32, (sub, sub), 1)
    suffix = jnp.where(row > col, 1.0, 0.0).astype(BF16)

    n_sub = tq // sub

    def weigh(j, slot, diag):
        start = pl.multiple_of(j * tq, tq)
        z = _nt_dot(q2, k_ref[0, pl.ds(start, tq), :])
        neg_abs = lax.bitcast_convert_type(lax.bitcast_convert_type(z, jnp.int32) | jnp.int32(-2 ** 31), F32)
        log_beta = jnp.minimum(z, 0.0) - jnp.log2(1.0 + jnp.exp2(neg_abs))
        l1m = log_beta - z
        if diag:
            qrow = lax.broadcasted_iota(jnp.int32, (2 * tq, tq), 0) & (tq - 1)
            before = lax.broadcasted_iota(jnp.int32, (2 * tq, tq), 1) < qrow
            l1m = jnp.where(before, l1m, 0.0)
            log_beta = jnp.where(before, log_beta, NEG_INF)
        lb = l1m.astype(BF16)
        sums = []
        for s in range(n_sub):
            blk = slice(s * sub, (s + 1) * sub)
            g_scr[slot, :, blk] = log_beta[:, blk] + jnp.dot(lb[:, blk], suffix, preferred_element_type=F32)
            sums.append(jnp.sum(l1m[:, blk], axis=1, keepdims=True))
        return tuple(sums)

    def gather(j, slot, sums, c, acc):
        start = pl.multiple_of(j * tq, tq)
        parts = [None] * n_sub
        for s in reversed(range(n_sub)):
            parts[s] = jnp.exp2(g_scr[slot, :, s * sub:(s + 1) * sub] + c).astype(BF16)
            c = c + sums[s]
        a = jnp.concatenate(parts, axis=1)
        acc = acc + jnp.dot(a, v_ref[0, pl.ds(start, tq), :], preferred_element_type=F32)
        return c, acc

    odd = i % 2
    sums = weigh(i, odd, True)
    carry = (sums, jnp.zeros((2 * tq, 1), F32), jnp.zeros((2 * tq, LANES), F32))

    def single(n, cr):
        c1, a1 = gather(i, 1, cr[0], cr[1], cr[2])
        return weigh(i - 1, 0, False), c1, a1

    def double(n, cr):
        j = i - odd - 2 * n
        c1, a1 = gather(j, 0, cr[0], cr[1], cr[2])
        s1 = weigh(j - 1, 1, False)
        c2, a2 = gather(j - 1, 1, s1, c1, a1)
        return weigh(j - 2, 0, False), c2, a2

    carry = lax.fori_loop(0, odd, single, carry)
    sums, c, acc = lax.fori_loop(0, (i - odd) // 2, double, carry)
    _, acc = gather(0, 0, sums, c, acc)
    o_ref[0] = jnp.where(lane < HEAD_DIM, acc[:tq], acc[tq:])


def _sb_kernel_t(q_ref, k_ref, v_ref, o_ref, g_scr, *, tq):
    i = pl.program_id(2)
    dim = lax.broadcasted_iota(jnp.int32, (LANES, tq), 0)
    q_t = q_ref[0].astype(F32).T
    zero = jnp.zeros_like(q_t)
    q2_t = jnp.concatenate([jnp.where(dim < HEAD_DIM, q_t, zero), jnp.where(dim >= HEAD_DIM, q_t, zero)],
                           axis=1).astype(BF16)
    sub = SB_SUB
    n_sub = tq // sub
    r = lax.broadcasted_iota(jnp.int32, (sub + 8, sub), 0)
    c_ = lax.broadcasted_iota(jnp.int32, (sub + 8, sub), 1)
    sfx = jnp.where((c_ > r) | (r >= sub), 1.0, 0.0).astype(BF16)

    def weigh(j, slot, diag):
        start = pl.multiple_of(j * tq, tq)
        z = jnp.dot(k_ref[0, pl.ds(start, tq), :], q2_t, preferred_element_type=F32)
        neg_abs = lax.bitcast_convert_type(lax.bitcast_convert_type(z, jnp.int32) | jnp.int32(-2 ** 31), F32)
        log_beta = jnp.minimum(z, 0.0) - jnp.log2(1.0 + jnp.exp2(neg_abs))
        l1m = log_beta - z
        if diag:
            before = lax.broadcasted_iota(jnp.int32, (tq, 2 * tq), 0) < (
                lax.broadcasted_iota(jnp.int32, (tq, 2 * tq), 1) & (tq - 1))
            l1m = jnp.where(before, l1m, 0.0)
            log_beta = jnp.where(before, log_beta, NEG_INF)
        lb = l1m.astype(BF16)
        sums = []
        for s in range(n_sub):
            rows = slice(s * sub, (s + 1) * sub)
            ext = jnp.dot(sfx, lb[rows], preferred_element_type=F32)
            g_scr[slot, rows, :] = log_beta[rows] + ext[:sub]
            sums.append(ext[sub:sub + 1])
        return tuple(sums)

    def gather(j, slot, sums, c, acc):
        start = pl.multiple_of(j * tq, tq)
        parts = [None] * n_sub
        for s in reversed(range(n_sub)):
            parts[s] = jnp.exp2(g_scr[slot, s * sub:(s + 1) * sub, :] + c).astype(BF16)
            c = c + sums[s]
        a = jnp.concatenate(parts, axis=0)
        v_t = v_ref[0, pl.ds(start, tq), :].astype(F32).T.astype(BF16)
        acc = acc + jnp.dot(v_t, a, preferred_element_type=F32)
        return c, acc

    odd = i % 2
    sums = weigh(i, odd, True)
    carry = (sums, jnp.zeros((1, 2 * tq), F32), jnp.zeros((LANES, 2 * tq), F32))

    def single(n, cr):
        c1, a1 = gather(i, 1, cr[0], cr[1], cr[2])
        return weigh(i - 1, 0, False), c1, a1

    def alive(c):
        return (jnp.max(c) > SB_DEAD).astype(jnp.int32)

    def double(cr):
        n, _, sums, c, acc = cr
        j = i - odd - 2 * n
        c1, a1 = gather(j, 0, sums, c, acc)
        s1 = weigh(j - 1, 1, False)

        def rest(_):
            c2, a2 = gather(j - 1, 1, s1, c1, a1)
            return alive(c2), weigh(j - 2, 0, False), c2, a2

        live, s2, c2, a2 = lax.cond(alive(c1) > 0, rest, lambda _: (jnp.int32(0), s1, c1, a1), None)
        return n + 1, live, s2, c2, a2

    n_trips = (i - odd) // 2
    sums, c, acc = lax.fori_loop(0, odd, single, carry)
    n, live, sums, c, acc = lax.while_loop(lambda cr: (cr[0] < n_trips) & (cr[1] > 0), double,
                                           (jnp.int32(0), alive(c), sums, c, acc))
    acc = lax.cond(live > 0, lambda _: gather(i - odd - 2 * n, 0, sums, c, acc)[1], lambda _: acc, None)
    o_ref[0] = jnp.where(dim < HEAD_DIM, acc[:, :tq], acc[:, tq:]).T


def _sb_attention(q, k, v, tq=ATTN_TILE):
    B, S, _ = q.shape
    grid = (B, PAIRS, S // tq)
    return pl.pallas_call(
        functools.partial(_sb_kernel_t, tq=tq), grid=grid,
        in_specs=[pl.BlockSpec((1, tq, LANES), lambda b, p, i: (b, i, p)),
                  pl.BlockSpec((1, S, LANES), lambda b, p, i: (b, 0, p)),
                  pl.BlockSpec((1, S, LANES), lambda b, p, i: (b, 0, p))],
        out_specs=pl.BlockSpec((1, tq, LANES), lambda b, p, i: (b, i, p)),
        out_shape=jax.ShapeDtypeStruct((B, S, D_INNER), F32),
        scratch_shapes=[pltpu.VMEM((2, tq, 2 * tq), F32)],
        compiler_params=_cparams(3), name="sb_attn")(q, k, v)


def _moba_kernel_t(q_ref, k_ref, v_ref, km_ref, o_ref, s_scr, *, nkb, tq):
    i = pl.program_id(2)
    km = km_ref[0]
    per_tile = tq // MOBA_BLOCK
    dim = lax.broadcasted_iota(jnp.int32, (LANES, tq), 0)
    q_t = q_ref[0].astype(F32).T
    zero = jnp.zeros_like(q_t)
    q2_t = jnp.concatenate([jnp.where(dim < HEAD_DIM, q_t, zero), jnp.where(dim >= HEAD_DIM, q_t, zero)],
                           axis=1).astype(BF16)

    km_hi = km.astype(BF16)
    km_lo = (km - km_hi.astype(F32)).astype(BF16)
    gate = jnp.dot(jnp.concatenate([km_hi, km_lo], axis=1), jnp.concatenate([q2_t, q2_t], axis=0),
                   preferred_element_type=F32)
    blk = lax.broadcasted_iota(jnp.int32, (nkb, 2 * tq), 0)
    blk_f = blk.astype(F32)
    qcol = lax.broadcasted_iota(jnp.int32, (nkb, 2 * tq), 1) & (tq - 1)
    q_blk = i * per_tile + qcol // MOBA_BLOCK
    past = blk < q_blk
    g = jnp.where(past, gate, NEG_INF)
    sel = jnp.zeros((nkb, 2 * tq), jnp.bool_)
    for _ in range(MOBA_TOPK):
        mx = jnp.max(g, axis=0, keepdims=True)
        first = jnp.min(jnp.where(g == mx, blk_f, float(nkb)), axis=0, keepdims=True)
        pick = blk_f == first
        sel = sel | (pick & past)
        g = jnp.where(pick, NEG_INF, g)
    bias = jnp.where(sel | (blk == q_blk), 0.0, MASKED).astype(BF16)
    qx_t = jnp.concatenate([q2_t, bias, jnp.zeros((LANES - nkb, 2 * tq), BF16)], axis=0)
    lane_k = lax.broadcasted_iota(jnp.int32, (tq, LANES), 1)
    key_blk = lax.broadcasted_iota(jnp.int32, (tq, LANES), 0) // MOBA_BLOCK

    def score(j, slot, diag):
        st = pl.multiple_of(j * tq, tq)
        onehot = jnp.where(lane_k == j * per_tile + key_blk, 1.0, 0.0).astype(BF16)
        kx = jnp.concatenate([k_ref[0, pl.ds(st, tq), :], onehot], axis=1)
        s = jnp.dot(kx, qx_t, preferred_element_type=F32)
        if diag:
            causal = lax.broadcasted_iota(jnp.int32, (tq, 2 * tq), 0) <= (
                lax.broadcasted_iota(jnp.int32, (tq, 2 * tq), 1) & (tq - 1))
            s = jnp.where(causal, s, NEG_INF)
        s_scr[slot] = s
        return jnp.max(s, axis=0, keepdims=True)

    def absorb(j, slot, m_tile, carry):
        m, l, acc = carry
        st = pl.multiple_of(j * tq, tq)
        v_t = v_ref[0, pl.ds(st, tq), :].astype(F32).T.astype(BF16)
        m_new = jnp.maximum(m, m_tile)
        alpha = jnp.exp2(m - m_new)
        pb = jnp.exp2(s_scr[slot] - m_new).astype(BF16)
        ones = jnp.ones((16, tq), BF16)
        res = [jnp.dot(jnp.concatenate([v_t[h * HEAD_DIM:(h + 1) * HEAD_DIM], ones], axis=0),
                       pb[:, h * tq:(h + 1) * tq], preferred_element_type=F32) for h in range(2)]
        l = alpha * l + jnp.concatenate([r_[HEAD_DIM:HEAD_DIM + 1] for r_ in res], axis=1)
        acc = alpha * acc + jnp.concatenate([r_[:HEAD_DIM] for r_ in res], axis=1)
        return m_new, l, acc

    odd = i % 2
    m_tile = score(i, odd, True)
    state = (jnp.full((1, 2 * tq), NEG_INF, F32), jnp.zeros((1, 2 * tq), F32), jnp.zeros((HEAD_DIM, 2 * tq), F32))

    def single(n, cr):
        st1 = absorb(i, 1, cr[0], cr[1])
        return score(i - 1, 0, False), st1

    def double(n, cr):
        j = i - odd - 2 * n
        st1 = absorb(j, 0, cr[0], cr[1])
        m1 = score(j - 1, 1, False)
        st2 = absorb(j - 1, 1, m1, st1)
        return score(j - 2, 0, False), st2

    carry = lax.fori_loop(0, odd, single, (m_tile, state))
    m_tile, state = lax.fori_loop(0, (i - odd) // 2, double, carry)
    _, l, acc = absorb(0, 0, m_tile, state)
    out = acc / l
    o_ref[0] = jnp.concatenate([out[:, :tq], out[:, tq:]], axis=0).T


def _moba_attention(q, k, v, kmean, tq=ATTN_TILE):
    B, S, _ = q.shape
    nkb = S // MOBA_BLOCK
    grid = (B, PAIRS, S // tq)
    return pl.pallas_call(
        functools.partial(_moba_kernel_t, nkb=nkb, tq=tq), grid=grid,
        in_specs=[pl.BlockSpec((1, tq, LANES), lambda b, p, i: (b, i, p)),
                  pl.BlockSpec((1, S, LANES), lambda b, p, i: (b, 0, p)),
                  pl.BlockSpec((1, S, LANES), lambda b, p, i: (b, 0, p)),
                  pl.BlockSpec((1, nkb, LANES), lambda b, p, i: (b, 0, p))],
        out_specs=pl.BlockSpec((1, tq, LANES), lambda b, p, i: (b, i, p)),
        out_shape=jax.ShapeDtypeStruct((B, S, D_INNER), F32),
        scratch_shapes=[pltpu.VMEM((2, tq, 2 * tq), F32)],
        compiler_params=_cparams(3), name="moba_attn")(q, k, v, kmean)


def _band_kernel(*refs, max_back, kv_heads, use_sinks, want_lse):
    tq = BAND_BLOCK
    pos = 0
    if use_sinks:
        sink_ref = refs[0]
        pos = 1
    q_ref, kp_ref, kc_ref, vp_ref, vc_ref = refs[pos:pos + 5]
    o_ref = refs[pos + 5]
    lse_ref = refs[pos + 6] if want_lse else None
    i = pl.program_id(2)
    rep = N_HEADS // kv_heads

    lane = lax.broadcasted_iota(jnp.int32, (tq, LANES), 1)
    row = lax.broadcasted_iota(jnp.int32, (tq, 2 * tq), 0)
    col = lax.broadcasted_iota(jnp.int32, (tq, 2 * tq), 1)
    dist = row - col + tq
    first_key = jnp.where(i > 0, 0, tq)
    valid = (dist >= 0) & (dist <= max_back) & (col >= first_key)
    low_half = lane < HEAD_DIM

    kcat = jnp.concatenate([kp_ref[0], kc_ref[0]], axis=0)
    vcat = jnp.concatenate([vp_ref[0], vc_ref[0]], axis=0)
    lse_acc = jnp.zeros((tq, LANES), F32)
    for pr in range(PAIRS):
        qf = q_ref[0, :, pr * LANES:(pr + 1) * LANES].astype(F32)
        q_same = qf.astype(BF16)
        q_swap = pltpu.roll(qf, HEAD_DIM, 1).astype(BF16)
        outs = []
        for hh in range(2):
            h = 2 * pr + hh
            g = h // rep
            gh = g % 2
            qsrc = q_same if gh == hh else q_swap
            qm = jnp.where(low_half == (gh == 0), qsrc, jnp.zeros_like(qsrc))
            kt = kcat[:, (g // 2) * LANES:(g // 2 + 1) * LANES]
            vt = vcat[:, (g // 2) * LANES:(g // 2 + 1) * LANES]
            s = jnp.where(valid, _nt_dot(qm, kt), NEG_INF)
            m = jnp.max(s, axis=1, keepdims=True)
            if use_sinks:
                sink = sink_ref[h]
                m = jnp.maximum(m, sink)
            e = jnp.exp(s - m)
            den = jnp.sum(e, axis=1, keepdims=True)
            if use_sinks:
                den = den + jnp.exp(sink - m)
            o = jnp.dot(e.astype(BF16), vt, preferred_element_type=F32) / den
            if gh != hh:
                o = pltpu.roll(o, HEAD_DIM, 1)
            outs.append(o)
            if want_lse:
                lse_acc = jnp.where(lane == h, m + jnp.log(den), lse_acc)
        o_ref[0, :, pr * LANES:(pr + 1) * LANES] = jnp.where(low_half, outs[0], outs[1])
    if want_lse:
        lse_ref[0] = lse_acc


def _band_attention(q, k, v, *, dilation, max_back, kv_heads, sinks=None, want_lse=False):
    B, L, _ = q.shape
    kvw = kv_heads * HEAD_DIM
    tq = BAND_BLOCK
    qv, kv_, vv = q, k, v
    grid = (B, dilation, L // tq)
    cur = lambda b, r, i: (b, i, r)
    prev = lambda b, r, i: (b, jnp.maximum(i - 1, 0), r)
    in_specs = [pl.BlockSpec((1, tq, D_INNER), cur),
                pl.BlockSpec((1, tq, kvw), prev), pl.BlockSpec((1, tq, kvw), cur),
                pl.BlockSpec((1, tq, kvw), prev), pl.BlockSpec((1, tq, kvw), cur)]
    args = [qv, kv_, kv_, vv, vv]
    if sinks is not None:
        in_specs = [pl.BlockSpec(memory_space=pltpu.SMEM)] + in_specs
        args = [sinks] + args
    out_shape = [jax.ShapeDtypeStruct((B, L, dilation * D_INNER), F32)]
    out_specs = [pl.BlockSpec((1, tq, D_INNER), cur)]
    if want_lse:
        out_shape.append(jax.ShapeDtypeStruct((B, L, dilation * LANES), F32))
        out_specs.append(pl.BlockSpec((1, tq, LANES), cur))
    kern = functools.partial(_band_kernel, max_back=max_back, kv_heads=kv_heads,
                             use_sinks=sinks is not None, want_lse=want_lse)
    res = pl.pallas_call(kern, grid=grid, in_specs=in_specs, out_specs=out_specs, out_shape=out_shape,
                         compiler_params=_cparams(3), name="band_attn")(*args)
    return tuple(res) if want_lse else res[0]


def _out_kernel(*refs, n_groups, dils):
    ys = refs[:n_groups]
    pos = n_groups
    if n_groups > 1:
        lses = refs[pos:pos + n_groups]
        expand_ref = refs[pos + n_groups]
        pos += n_groups + 1
    z_ref, x_ref, w_ref, g_ref, b_ref, o_ref = refs[pos:pos + 6]
    y_scr, l_scr = refs[pos + 6:pos + 8] if any(d > 1 for d in dils) else (None, None)

    def token_order(ref, scr, d, width):
        if d == 1:
            return ref[...]
        rows = ref.shape[0]
        tiles = width // LANES
        for r in range(d):
            for t in range(tiles):
                scr[t, pl.ds(r, rows, stride=d), :] = ref[:, r * width + t * LANES:r * width + (t + 1) * LANES]
        return jnp.concatenate([scr[t] for t in range(tiles)], axis=1) if tiles > 1 else scr[0]

    if n_groups == 1:
        y = ys[0][...]
    else:
        ls = [token_order(r, l_scr, d, LANES) for r, d in zip(lses, dils)]
        mx = functools.reduce(jnp.maximum, ls)
        es = [jnp.exp(l - mx) for l in ls]
        tot = functools.reduce(lambda a, b: a + b, es)
        y = None
        for e, yr, d in zip(es, ys, dils):
            wts = jnp.dot(e / tot, expand_ref[...], precision=lax.Precision.HIGHEST,
                          preferred_element_type=F32)
            yg = wts * token_order(yr, y_scr, d, D_INNER)
            y = yg if y is None else y + yg
    z = z_ref[...]
    u = (y * (z * jax.nn.sigmoid(z))).astype(BF16)
    r = DN_ALPHA * x_ref[...] + jnp.dot(u, w_ref[...], preferred_element_type=F32)
    mu = jnp.mean(r, axis=1, keepdims=True)
    d = r - mu
    var = jnp.mean(d * d, axis=1, keepdims=True)
    o_ref[...] = d * lax.rsqrt(var + LN_EPS) * g_ref[...] + b_ref[...]


def _out_block(ys, lses, z, x2, w_out_bf, g, b, dils=None, tm=256):
    T, D = x2.shape
    n_groups = len(ys)
    dils = dils or (1,) * n_groups
    row = pl.BlockSpec((tm, D), lambda i: (i, 0))
    in_specs = [pl.BlockSpec((tm // d, d * D_INNER), lambda i: (i, 0)) for d in dils]
    args = list(ys)
    if n_groups > 1:
        in_specs += [pl.BlockSpec((tm // d, d * LANES), lambda i: (i, 0)) for d in dils]
        args += list(lses)
        expand = (np.arange(LANES)[:, None] == (np.arange(D_INNER) // HEAD_DIM)[None, :]).astype(np.float32)
        in_specs.append(pl.BlockSpec((LANES, D_INNER), lambda i: (0, 0)))
        args.append(jnp.asarray(expand))
    in_specs += [row, row, pl.BlockSpec((D_INNER, D), lambda i: (0, 0)),
                 pl.BlockSpec((1, D), lambda i: (0, 0)), pl.BlockSpec((1, D), lambda i: (0, 0))]
    args += [z, x2, w_out_bf, g.reshape(1, D), b.reshape(1, D)]
    scratch = ([pltpu.VMEM((D_INNER // LANES, tm, LANES), F32), pltpu.VMEM((1, tm, LANES), F32)]
               if any(d > 1 for d in dils) else [])
    return pl.pallas_call(
        functools.partial(_out_kernel, n_groups=n_groups, dils=dils), grid=(T // tm,),
        in_specs=in_specs, out_specs=row, out_shape=jax.ShapeDtypeStruct((T, D), F32),
        scratch_shapes=scratch, compiler_params=_cparams(1), name="out_ln")(*args)


def kernel(x, sb_w_in, sb_w_out, ln0_g, ln0_b, moba_w_in, moba_w_out, ln1_g, ln1_b,
           swa_w_in, swa_sinks, swa_w_out, ln2_g, ln2_b, dil_w_in, dil_w_out, ln3_g, ln3_b):
    B, S, D = x.shape
    T = B * S
    x2 = x.reshape(T, D)
    rope = _rope_lane_tables(S)
    W = D_INNER

    segs = ((0, W, False, Q_SCALE_LOG2, 0), (W, W, False, 1.0, 1), (2 * W, W, False, 1.0, 2),
            (3 * W, W, False, 1.0, 3))
    q, k, v, z = _project(x2, sb_w_in.astype(BF16), segs, (W, W, W, W), (BF16, BF16, BF16, F32), tm=PROJ_ROWS_WIDE)
    y = _sb_attention(q.reshape(B, S, W), k.reshape(B, S, W), v.reshape(B, S, W))
    x2 = _out_block([y.reshape(T, W)], None, z, x2, sb_w_out.astype(BF16), ln0_g, ln0_b)

    segs = ((0, W, True, Q_SCALE_LOG2, 0), (W, W, True, 1.0, 1), (2 * W, W, False, 1.0, 2),
            (3 * W, W, False, 1.0, 3))
    q, k, v, z, kmean = _project(x2, moba_w_in.astype(BF16), segs, (W, W, W, W), (BF16, BF16, BF16, F32),
                                 rope_tabs=rope, kmean_seg=1, tm=PROJ_ROWS_WIDE)
    y = _moba_attention(q.reshape(B, S, W), k.reshape(B, S, W), v.reshape(B, S, W),
                        kmean.reshape(B, S // MOBA_BLOCK, W))
    x2 = _out_block([y.reshape(T, W)], None, z, x2, moba_w_out.astype(BF16), ln1_g, ln1_b)

    kvw = SWA_KV_HEADS * HEAD_DIM
    segs = ((0, W, True, Q_SCALE, 0), (W, kvw, True, 1.0, 1), (W + kvw, kvw, False, 1.0, 2),
            (W + 2 * kvw, W, False, 1.0, 3))
    q, k, v, z = _project(x2, swa_w_in.astype(BF16), segs, (W, kvw, kvw, W), (BF16, BF16, BF16, F32),
                          rope_tabs=rope, tm=PROJ_ROWS_WIDE)
    y = _band_attention(q.reshape(B, S, W), k.reshape(B, S, kvw), v.reshape(B, S, kvw), dilation=1,
                        max_back=SWA_WINDOW - 1, kv_heads=SWA_KV_HEADS, sinks=swa_sinks.astype(F32))
    x2 = _out_block([y.reshape(T, W)], None, z, x2, swa_w_out.astype(BF16), ln2_g, ln2_b)

    n_g = len(DILATED_GROUPS)
    segs = []
    for g in range(n_g):
        segs += [((3 * g) * W, W, True, Q_SCALE, 3 * g), ((3 * g + 1) * W, W, True, 1.0, 3 * g + 1),
                 ((3 * g + 2) * W, W, False, 1.0, 3 * g + 2)]
    segs.append((3 * n_g * W, W, False, 1.0, 3 * n_g))
    dils = tuple(d for _, d in DILATED_GROUPS)
    out_dils = tuple(d for d in dils for _ in range(3)) + (1,)
    outs = _project(x2, dil_w_in.astype(BF16), tuple(segs), (W,) * (3 * n_g + 1), (BF16,) * (3 * n_g) + (F32,),
                    rope_tabs=rope, out_dils=out_dils)
    z = outs[-1]
    ys, lses = [], []
    for g, (window, dil) in enumerate(DILATED_GROUPS):
        qg, kg, vg = (outs[3 * g + t].reshape(B, S // dil, dil * W) for t in range(3))
        o, lse = _band_attention(qg, kg, vg, dilation=dil, max_back=window // dil, kv_heads=N_HEADS,
                                 want_lse=True)
        ys.append(o.reshape(T // dil, dil * W))
        lses.append(lse.reshape(T // dil, dil * LANES))
    x2 = _out_block(ys, lses, z, x2, dil_w_out.astype(BF16), ln3_g, ln3_b, dils=dils)
    return x2.reshape(B, S, D)
```

```python
import functools

import jax
import jax.numpy as jnp
import numpy as np
from jax import lax
from jax.experimental import pallas as pl
from jax.experimental.pallas import tpu as pltpu

D_MODEL = 1024
HEAD_DIM = 64
N_HEADS = D_MODEL // HEAD_DIM
D_INNER = N_HEADS * HEAD_DIM
ROPE_THETA = 500000.0
ROT_DIM = HEAD_DIM // 4
LN_EPS = 1e-5
DEPTH = 4
DN_ALPHA = (2.0 * DEPTH) ** 0.25
MOBA_BLOCK = 256
MOBA_TOPK = 3
SWA_WINDOW = 128
SWA_KV_HEADS = 4
DILATED_GROUPS = ((128, 1), (512, 4), (2048, 16))
BAND_BLOCK = 128
ATTN_TILE = 512
PROJ_ROWS_WIDE = 512
SB_SUB = 256
SB_DEAD = -160.0
Q_SCALE = HEAD_DIM ** -0.5
Q_SCALE_LOG2 = Q_SCALE * float(np.log2(np.e))

LANES = 128
PAIRS = D_INNER // LANES
VMEM_LIMIT = 56 * 1024 * 1024

F32 = jnp.float32
BF16 = jnp.bfloat16
NEG_INF = float("-inf")
MASKED = -2.0 ** 60


def _cparams(n_axes):
    return pltpu.CompilerParams(dimension_semantics=("arbitrary",) * n_axes, vmem_limit_bytes=VMEM_LIMIT)


def _nt_dot(a, b):
    return lax.dot_general(a, b, (((1,), (1,)), ((), ())), preferred_element_type=F32)


def _rope_lane_tables(seq_len):
    half = ROT_DIM // 2
    pos = jnp.arange(seq_len, dtype=F32)
    inv = ROPE_THETA ** (-jnp.arange(0, ROT_DIM, 2, dtype=F32) / ROT_DIM)
    ang = pos[:, None] * inv[None, :]
    cos, sin = jnp.cos(ang), jnp.sin(ang)
    hl = np.arange(LANES) % HEAD_DIM
    idx = jnp.asarray(hl % half)
    cos_l, sin_l = cos[:, idx], sin[:, idx]
    c = jnp.where(jnp.asarray(hl < ROT_DIM)[None, :], cos_l, 1.0)
    s1 = jnp.where(jnp.asarray(hl < half)[None, :], -sin_l, 0.0)
    s2 = jnp.where(jnp.asarray((hl >= half) & (hl < ROT_DIM))[None, :], sin_l, 0.0)
    return c.astype(F32), s1.astype(F32), s2.astype(F32)


def _apply_rope(y, c, s1, s2):
    parts = []
    for t in range(y.shape[1] // LANES):
        yt = y[:, t * LANES:(t + 1) * LANES]
        up = pltpu.roll(yt, LANES - ROT_DIM // 2, 1)
        dn = pltpu.roll(yt, ROT_DIM // 2, 1)
        parts.append(yt * c + up * s1 + dn * s2)
    return parts[0] if len(parts) == 1 else jnp.concatenate(parts, axis=1)


def _proj_kernel(*refs, segs, use_rope, kmean_seg, out_dils):
    x_ref, w_ref = refs[0], refs[1]
    pos = 2
    if use_rope:
        c, s1, s2 = refs[2][...], refs[3][...], refs[4][...]
        pos = 5
    streamed = any(d > 1 for d in out_dils)
    outs = refs[pos:-1] if streamed else refs[pos:]
    xb = x_ref[...].astype(BF16)
    for si, (col0, width, rope, scale, oi) in enumerate(segs):
        y = jnp.dot(xb, w_ref[:, col0:col0 + width], preferred_element_type=F32)
        if rope:
            y = _apply_rope(y, c, s1, s2)
        if scale != 1.0:
            y = y * scale
        d = out_dils[oi]
        if d == 1:
            outs[oi][...] = y.astype(outs[oi].dtype)
        else:
            y_scr = refs[-1]
            rows = y.shape[0] // d
            for t in range(width // LANES):
                y_scr[t] = y[:, t * LANES:(t + 1) * LANES]
            for r in range(d):
                for t in range(width // LANES):
                    lanes = slice(r * width + t * LANES, r * width + (t + 1) * LANES)
                    outs[oi][:, lanes] = y_scr[t, pl.ds(r, rows, stride=d), :].astype(outs[oi].dtype)
        if kmean_seg == si:
            tm = y.shape[0]
            km = outs[-1]
            for blk in range(tm // MOBA_BLOCK):
                rows = y[blk * MOBA_BLOCK:(blk + 1) * MOBA_BLOCK, :]
                km[0, blk:blk + 1, :] = jnp.sum(rows, axis=0, keepdims=True) * (1.0 / MOBA_BLOCK)


def _project(x2, w_bf, segs, out_widths, out_dtypes, rope_tabs=None, kmean_seg=None, out_dils=None, tm=256):
    T, D = x2.shape
    N = w_bf.shape[1]
    out_dils = out_dils or (1,) * len(out_widths)
    seq = rope_tabs[0].shape[0] if rope_tabs is not None else None
    in_specs = [pl.BlockSpec((tm, D), lambda i: (i, 0)),
                pl.BlockSpec((D, N), lambda i: (0, 0), pipeline_mode=pl.Buffered(1))]
    args = [x2, w_bf]
    if rope_tabs is not None:
        nblk = seq // tm
        for t in rope_tabs:
            in_specs.append(pl.BlockSpec((tm, LANES), lambda i: (i % nblk, 0)))
            args.append(t)
    out_shape = [jax.ShapeDtypeStruct((T // d, d * w), dt) for w, dt, d in zip(out_widths, out_dtypes, out_dils)]
    out_specs = [pl.BlockSpec((tm // d, d * w), lambda i: (i, 0)) for w, d in zip(out_widths, out_dils)]
    if kmean_seg is not None:
        nb = tm // MOBA_BLOCK
        out_shape.append(jax.ShapeDtypeStruct((T // tm, nb, D_INNER), F32))
        out_specs.append(pl.BlockSpec((1, nb, D_INNER), lambda i: (i, 0, 0)))
    scratch = [pltpu.VMEM((max(out_widths) // LANES, tm, LANES), F32)] if any(d > 1 for d in out_dils) else []
    kern = functools.partial(_proj_kernel, segs=segs, use_rope=rope_tabs is not None, kmean_seg=kmean_seg,
                             out_dils=out_dils)
    return pl.pallas_call(
        kern, grid=(T // tm,), in_specs=in_specs, out_specs=out_specs, out_shape=out_shape,
        scratch_shapes=scratch, compiler_params=_cparams(1), name="in_proj")(*args)


def _sb_kernel(q_ref, k_ref, v_ref, o_ref, g_scr, *, tq):
    i = pl.program_id(2)
    q = q_ref[0]
    lane = lax.broadcasted_iota(jnp.int32, (tq, LANES), 1)
    zero = jnp.zeros_like(q)
    q2 = jnp.concatenate([jnp.where(lane < HEAD_DIM, q, zero), jnp.where(lane >= HEAD_DIM, q, zero)], axis=0)
    sub = SB_SUB
    row = lax.broadcasted_iota(jnp.int32, (sub, sub), 0)
    col = lax.broadcasted_iota(jnp.int32, (sub, sub), 1)
    suffix = jnp.where(row > col, 1.0, 0.0).astype(BF16)

    n_sub = tq // sub

    def weigh(j, slot, diag):
        start = pl.multiple_of(j * tq, tq)
        z = _nt_dot(q2, k_ref[0, pl.ds(start, tq), :])
        neg_abs = lax.bitcast_convert_type(lax.bitcast_convert_type(z, jnp.int32) | jnp.int32(-2 ** 31), F32)
        log_beta = jnp.minimum(z, 0.0) - jnp.log2(1.0 + jnp.exp2(neg_abs))
        l1m = log_beta - z
        if diag:
            qrow = lax.broadcasted_iota(jnp.int32, (2 * tq, tq), 0) & (tq - 1)
            before = lax.broadcasted_iota(jnp.int32, (2 * tq, tq), 1) < qrow
            l1m = jnp.where(before, l1m, 0.0)
            log_beta = jnp.where(before, log_beta, NEG_INF)
        lb = l1m.astype(BF16)
        sums = []
        for s in range(n_sub):
            blk = slice(s * sub, (s + 1) * sub)
            g_scr[slot, :, blk] = log_beta[:, blk] + jnp.dot(lb[:, blk], suffix, preferred_element_type=F32)
            sums.append(jnp.sum(l1m[:, blk], axis=1, keepdims=True))
        return tuple(sums)

    def gather(j, slot, sums, c, acc):
        start = pl.multiple_of(j * tq, tq)
        parts = [None] * n_sub
        for s in reversed(range(n_sub)):
            parts[s] = jnp.exp2(g_scr[slot, :, s * sub:(s + 1) * sub] + c).astype(BF16)
            c = c + sums[s]
        a = jnp.concatenate(parts, axis=1)
        acc = acc + jnp.dot(a, v_ref[0, pl.ds(start, tq), :], preferred_element_type=F32)
        return c, acc

    odd = i % 2
    sums = weigh(i, odd, True)
    carry = (sums, jnp.zeros((2 * tq, 1), F32), jnp.zeros((2 * tq, LANES), F32))

    def single(n, cr):
        c1, a1 = gather(i, 1, cr[0], cr[1], cr[2])
        return weigh(i - 1, 0, False), c1, a1

    def double(n, cr):
        j = i - odd - 2 * n
        c1, a1 = gather(j, 0, cr[0], cr[1], cr[2])
        s1 = weigh(j - 1, 1, False)
        c2, a2 = gather(j - 1, 1, s1, c1, a1)
        return weigh(j - 2, 0, False), c2, a2

    carry = lax.fori_loop(0, odd, single, carry)
    sums, c, acc = lax.fori_loop(0, (i - odd) // 2, double, carry)
    _, acc = gather(0, 0, sums, c, acc)
    o_ref[0] = jnp.where(lane < HEAD_DIM, acc[:tq], acc[tq:])


def _sb_kernel_t(q_ref, k_ref, v_ref, o_ref, g_scr, *, tq):
    i = pl.program_id(2)
    dim = lax.broadcasted_iota(jnp.int32, (LANES, tq), 0)
    q_t = q_ref[0].astype(F32).T
    zero = jnp.zeros_like(q_t)
    q2_t = jnp.concatenate([jnp.where(dim < HEAD_DIM, q_t, zero), jnp.where(dim >= HEAD_DIM, q_t, zero)],
                           axis=1).astype(BF16)
    sub = SB_SUB
    n_sub = tq // sub
    r = lax.broadcasted_iota(jnp.int32, (sub + 8, sub), 0)
    c_ = lax.broadcasted_iota(jnp.int32, (sub + 8, sub), 1)
    sfx = jnp.where((c_ > r) | (r >= sub), 1.0, 0.0).astype(BF16)

    def weigh(j, slot, diag):
        start = pl.multiple_of(j * tq, tq)
        z = jnp.dot(k_ref[0, pl.ds(start, tq), :], q2_t, preferred_element_type=F32)
        neg_abs = lax.bitcast_convert_type(lax.bitcast_convert_type(z, jnp.int32) | jnp.int32(-2 ** 31), F32)
        log_beta = jnp.minimum(z, 0.0) - jnp.log2(1.0 + jnp.exp2(neg_abs))
        l1m = log_beta - z
        if diag:
            before = lax.broadcasted_iota(jnp.int32, (tq, 2 * tq), 0) < (
                lax.broadcasted_iota(jnp.int32, (tq, 2 * tq), 1) & (tq - 1))
            l1m = jnp.where(before, l1m, 0.0)
            log_beta = jnp.where(before, log_beta, NEG_INF)
        lb = l1m.astype(BF16)
        sums = []
        for s in range(n_sub):
            rows = slice(s * sub, (s + 1) * sub)
            ext = jnp.dot(sfx, lb[rows], preferred_element_type=F32)
            g_scr[slot, rows, :] = log_beta[rows] + ext[:sub]
            sums.append(ext[sub:sub + 1])
        return tuple(sums)

    def gather(j, slot, sums, c, acc):
        start = pl.multiple_of(j * tq, tq)
        parts = [None] * n_sub
        for s in reversed(range(n_sub)):
            parts[s] = jnp.exp2(g_scr[slot, s * sub:(s + 1) * sub, :] + c).astype(BF16)
            c = c + sums[s]
        a = jnp.concatenate(parts, axis=0)
        v_t = v_ref[0, pl.ds(start, tq), :].astype(F32).T.astype(BF16)
        acc = acc + jnp.dot(v_t, a, preferred_element_type=F32)
        return c, acc

    odd = i % 2
    sums = weigh(i, odd, True)
    carry = (sums, jnp.zeros((1, 2 * tq), F32), jnp.zeros((LANES, 2 * tq), F32))

    def single(n, cr):
        c1, a1 = gather(i, 1, cr[0], cr[1], cr[2])
        return weigh(i - 1, 0, False), c1, a1

    def alive(c):
        return (jnp.max(c) > SB_DEAD).astype(jnp.int32)

    def double(cr):
        n, _, sums, c, acc = cr
        j = i - odd - 2 * n
        c1, a1 = gather(j, 0, sums, c, acc)
        s1 = weigh(j - 1, 1, False)

        def rest(_):
            c2, a2 = gather(j - 1, 1, s1, c1, a1)
            return alive(c2), weigh(j - 2, 0, False), c2, a2

        live, s2, c2, a2 = lax.cond(alive(c1) > 0, rest, lambda _: (jnp.int32(0), s1, c1, a1), None)
        return n + 1, live, s2, c2, a2

    n_trips = (i - odd) // 2
    sums, c, acc = lax.fori_loop(0, odd, single, carry)
    n, live, sums, c, acc = lax.while_loop(lambda cr: (cr[0] < n_trips) & (cr[1] > 0), double,
                                           (jnp.int32(0), alive(c), sums, c, acc))
    acc = lax.cond(live > 0, lambda _: gather(i - odd - 2 * n, 0, sums, c, acc)[1], lambda _: acc, None)
    o_ref[0] = jnp.where(dim < HEAD_DIM, acc[:, :tq], acc[:, tq:]).T


def _sb_attention(q, k, v, tq=ATTN_TILE):
    B, S, _ = q.shape
    grid = (B, PAIRS, S // tq)
    return pl.pallas_call(
        functools.partial(_sb_kernel_t, tq=tq), grid=grid,
        in_specs=[pl.BlockSpec((1, tq, LANES), lambda b, p, i: (b, i, p)),
                  pl.BlockSpec((1, S, LANES), lambda b, p, i: (b, 0, p)),
                  pl.BlockSpec((1, S, LANES), lambda b, p, i: (b, 0, p))],
        out_specs=pl.BlockSpec((1, tq, LANES), lambda b, p, i: (b, i, p)),
        out_shape=jax.ShapeDtypeStruct((B, S, D_INNER), F32),
        scratch_shapes=[pltpu.VMEM((2, tq, 2 * tq), F32)],
        compiler_params=_cparams(3), name="sb_attn")(q, k, v)


def _moba_kernel_t(q_ref, k_ref, v_ref, km_ref, o_ref, s_scr, *, nkb, tq):
    i = pl.program_id(2)
    km = km_ref[0]
    per_tile = tq // MOBA_BLOCK
    dim = lax.broadcasted_iota(jnp.int32, (LANES, tq), 0)
    q_t = q_ref[0].astype(F32).T
    zero = jnp.zeros_like(q_t)
    q2_t = jnp.concatenate([jnp.where(dim < HEAD_DIM, q_t, zero), jnp.where(dim >= HEAD_DIM, q_t, zero)],
                           axis=1).astype(BF16)

    km_hi = km.astype(BF16)
    km_lo = (km - km_hi.astype(F32)).astype(BF16)
    gate = jnp.dot(jnp.concatenate([km_hi, km_lo], axis=1), jnp.concatenate([q2_t, q2_t], axis=0),
                   preferred_element_type=F32)
    blk = lax.broadcasted_iota(jnp.int32, (nkb, 2 * tq), 0)
    blk_f = blk.astype(F32)
    qcol = lax.broadcasted_iota(jnp.int32, (nkb, 2 * tq), 1) & (tq - 1)
    q_blk = i * per_tile + qcol // MOBA_BLOCK
    past = blk < q_blk
    g = jnp.where(past, gate, NEG_INF)
    sel = jnp.zeros((nkb, 2 * tq), jnp.bool_)
    for _ in range(MOBA_TOPK):
        mx = jnp.max(g, axis=0, keepdims=True)
        first = jnp.min(jnp.where(g == mx, blk_f, float(nkb)), axis=0, keepdims=True)
        pick = blk_f == first
        sel = sel | (pick & past)
        g = jnp.where(pick, NEG_INF, g)
    bias = jnp.where(sel | (blk == q_blk), 0.0, MASKED).astype(BF16)
    qx_t = jnp.concatenate([q2_t, bias, jnp.zeros((LANES - nkb, 2 * tq), BF16)], axis=0)
    lane_k = lax.broadcasted_iota(jnp.int32, (tq, LANES), 1)
    key_blk = lax.broadcasted_iota(jnp.int32, (tq, LANES), 0) // MOBA_BLOCK

    def score(j, slot, diag):
        st = pl.multiple_of(j * tq, tq)
        onehot = jnp.where(lane_k == j * per_tile + key_blk, 1.0, 0.0).astype(BF16)
        kx = jnp.concatenate([k_ref[0, pl.ds(st, tq), :], onehot], axis=1)
        s = jnp.dot(kx, qx_t, preferred_element_type=F32)
        if diag:
            causal = lax.broadcasted_iota(jnp.int32, (tq, 2 * tq), 0) <= (
                lax.broadcasted_iota(jnp.int32, (tq, 2 * tq), 1) & (tq - 1))
            s = jnp.where(causal, s, NEG_INF)
        s_scr[slot] = s
        return jnp.max(s, axis=0, keepdims=True)

    def absorb(j, slot, m_tile, carry):
        m, l, acc = carry
        st = pl.multiple_of(j * tq, tq)
        v_t = v_ref[0, pl.ds(st, tq), :].astype(F32).T.astype(BF16)
        m_new = jnp.maximum(m, m_tile)
        alpha = jnp.exp2(m - m_new)
        pb = jnp.exp2(s_scr[slot] - m_new).astype(BF16)
        ones = jnp.ones((16, tq), BF16)
        res = [jnp.dot(jnp.concatenate([v_t[h * HEAD_DIM:(h + 1) * HEAD_DIM], ones], axis=0),
                       pb[:, h * tq:(h + 1) * tq], preferred_element_type=F32) for h in range(2)]
        l = alpha * l + jnp.concatenate([r_[HEAD_DIM:HEAD_DIM + 1] for r_ in res], axis=1)
        acc = alpha * acc + jnp.concatenate([r_[:HEAD_DIM] for r_ in res], axis=1)
        return m_new, l, acc

    odd = i % 2
    m_tile = score(i, odd, True)
    state = (jnp.full((1, 2 * tq), NEG_INF, F32), jnp.zeros((1, 2 * tq), F32), jnp.zeros((HEAD_DIM, 2 * tq), F32))

    def single(n, cr):
        st1 = absorb(i, 1, cr[0], cr[1])
        return score(i - 1, 0, False), st1

    def double(n, cr):
        j = i - odd - 2 * n
        st1 = absorb(j, 0, cr[0], cr[1])
        m1 = score(j - 1, 1, False)
        st2 = absorb(j - 1, 1, m1, st1)
        return score(j - 2, 0, False), st2

    carry = lax.fori_loop(0, odd, single, (m_tile, state))
    m_tile, state = lax.fori_loop(0, (i - odd) // 2, double, carry)
    _, l, acc = absorb(0, 0, m_tile, state)
    out = acc / l
    o_ref[0] = jnp.concatenate([out[:, :tq], out[:, tq:]], axis=0).T


def _moba_attention(q, k, v, kmean, tq=ATTN_TILE):
    B, S, _ = q.shape
    nkb = S // MOBA_BLOCK
    grid = (B, PAIRS, S // tq)
    return pl.pallas_call(
        functools.partial(_moba_kernel_t, nkb=nkb, tq=tq), grid=grid,
        in_specs=[pl.BlockSpec((1, tq, LANES), lambda b, p, i: (b, i, p)),
                  pl.BlockSpec((1, S, LANES), lambda b, p, i: (b, 0, p)),
                  pl.BlockSpec((1, S, LANES), lambda b, p, i: (b, 0, p)),
                  pl.BlockSpec((1, nkb, LANES), lambda b, p, i: (b, 0, p))],
        out_specs=pl.BlockSpec((1, tq, LANES), lambda b, p, i: (b, i, p)),
        out_shape=jax.ShapeDtypeStruct((B, S, D_INNER), F32),
        scratch_shapes=[pltpu.VMEM((2, tq, 2 * tq), F32)],
        compiler_params=_cparams(3), name="moba_attn")(q, k, v, kmean)


def _band_kernel(*refs, max_back, kv_heads, use_sinks, want_lse):
    tq = BAND_BLOCK
    pos = 0
    if use_sinks:
        sink_ref = refs[0]
        pos = 1
    q_ref, kp_ref, kc_ref, vp_ref, vc_ref = refs[pos:pos + 5]
    o_ref = refs[pos + 5]
    lse_ref = refs[pos + 6] if want_lse else None
    i = pl.program_id(2)
    rep = N_HEADS // kv_heads

    lane = lax.broadcasted_iota(jnp.int32, (tq, LANES), 1)
    row = lax.broadcasted_iota(jnp.int32, (tq, 2 * tq), 0)
    col = lax.broadcasted_iota(jnp.int32, (tq, 2 * tq), 1)
    dist = row - col + tq
    first_key = jnp.where(i > 0, 0, tq)
    valid = (dist >= 0) & (dist <= max_back) & (col >= first_key)
    low_half = lane < HEAD_DIM

    kcat = jnp.concatenate([kp_ref[0], kc_ref[0]], axis=0)
    vcat = jnp.concatenate([vp_ref[0], vc_ref[0]], axis=0)
    lse_acc = jnp.zeros((tq, LANES), F32)
    for pr in range(PAIRS):
        qf = q_ref[0, :, pr * LANES:(pr + 1) * LANES].astype(F32)
        q_same = qf.astype(BF16)
        q_swap = pltpu.roll(qf, HEAD_DIM, 1).astype(BF16)
        outs = []
        for hh in range(2):
            h = 2 * pr + hh
            g = h // rep
            gh = g % 2
            qsrc = q_same if gh == hh else q_swap
            qm = jnp.where(low_half == (gh == 0), qsrc, jnp.zeros_like(qsrc))
            kt = kcat[:, (g // 2) * LANES:(g // 2 + 1) * LANES]
            vt = vcat[:, (g // 2) * LANES:(g // 2 + 1) * LANES]
            s = jnp.where(valid, _nt_dot(qm, kt), NEG_INF)
            m = jnp.max(s, axis=1, keepdims=True)
            if use_sinks:
                sink = sink_ref[h]
                m = jnp.maximum(m, sink)
            e = jnp.exp(s - m)
            den = jnp.sum(e, axis=1, keepdims=True)
            if use_sinks:
                den = den + jnp.exp(sink - m)
            o = jnp.dot(e.astype(BF16), vt, preferred_element_type=F32) / den
            if gh != hh:
                o = pltpu.roll(o, HEAD_DIM, 1)
            outs.append(o)
            if want_lse:
                lse_acc = jnp.where(lane == h, m + jnp.log(den), lse_acc)
        o_ref[0, :, pr * LANES:(pr + 1) * LANES] = jnp.where(low_half, outs[0], outs[1])
    if want_lse:
        lse_ref[0] = lse_acc


def _band_attention(q, k, v, *, dilation, max_back, kv_heads, sinks=None, want_lse=False):
    B, L, _ = q.shape
    kvw = kv_heads * HEAD_DIM
    tq = BAND_BLOCK
    qv, kv_, vv = q, k, v
    grid = (B, dilation, L // tq)
    cur = lambda b, r, i: (b, i, r)
    prev = lambda b, r, i: (b, jnp.maximum(i - 1, 0), r)
    in_specs = [pl.BlockSpec((1, tq, D_INNER), cur),
                pl.BlockSpec((1, tq, kvw), prev), pl.BlockSpec((1, tq, kvw), cur),
                pl.BlockSpec((1, tq, kvw), prev), pl.BlockSpec((1, tq, kvw), cur)]
    args = [qv, kv_, kv_, vv, vv]
    if sinks is not None:
        in_specs = [pl.BlockSpec(memory_space=pltpu.SMEM)] + in_specs
        args = [sinks] + args
    out_shape = [jax.ShapeDtypeStruct((B, L, dilation * D_INNER), F32)]
    out_specs = [pl.BlockSpec((1, tq, D_INNER), cur)]
    if want_lse:
        out_shape.append(jax.ShapeDtypeStruct((B, L, dilation * LANES), F32))
        out_specs.append(pl.BlockSpec((1, tq, LANES), cur))
    kern = functools.partial(_band_kernel, max_back=max_back, kv_heads=kv_heads,
                             use_sinks=sinks is not None, want_lse=want_lse)
    res = pl.pallas_call(kern, grid=grid, in_specs=in_specs, out_specs=out_specs, out_shape=out_shape,
                         compiler_params=_cparams(3), name="band_attn")(*args)
    return tuple(res) if want_lse else res[0]


def _out_kernel(*refs, n_groups, dils):
    ys = refs[:n_groups]
    pos = n_groups
    if n_groups > 1:
        lses = refs[pos:pos + n_groups]
        expand_ref = refs[pos + n_groups]
        pos += n_groups + 1
    z_ref, x_ref, w_ref, g_ref, b_ref, o_ref = refs[pos:pos + 6]
    y_scr, l_scr = refs[pos + 6:pos + 8] if any(d > 1 for d in dils) else (None, None)

    def token_order(ref, scr, d, width):
        if d == 1:
            return ref[...]
        rows = ref.shape[0]
        tiles = width // LANES
        for r in range(d):
            for t in range(tiles):
                scr[t, pl.ds(r, rows, stride=d), :] = ref[:, r * width + t * LANES:r * width + (t + 1) * LANES]
        return jnp.concatenate([scr[t] for t in range(tiles)], axis=1) if tiles > 1 else scr[0]

    if n_groups == 1:
        y = ys[0][...]
    else:
        ls = [token_order(r, l_scr, d, LANES) for r, d in zip(lses, dils)]
        mx = functools.reduce(jnp.maximum, ls)
        es = [jnp.exp(l - mx) for l in ls]
        tot = functools.reduce(lambda a, b: a + b, es)
        y = None
        for e, yr, d in zip(es, ys, dils):
            w = e / tot
            w_hi = w.astype(BF16)
            w_lo = (w - w_hi.astype(F32)).astype(BF16)
            wts = jnp.dot(jnp.concatenate([w_hi, w_lo], axis=1), expand_ref[...], preferred_element_type=F32)
            yg = wts * token_order(yr, y_scr, d, D_INNER)
            y = yg if y is None else y + yg
    z = z_ref[...]
    u = (y * (z * jax.nn.sigmoid(z))).astype(BF16)
    r = DN_ALPHA * x_ref[...] + jnp.dot(u, w_ref[...], preferred_element_type=F32)
    mu = jnp.mean(r, axis=1, keepdims=True)
    d = r - mu
    var = jnp.mean(d * d, axis=1, keepdims=True)
    o_ref[...] = d * lax.rsqrt(var + LN_EPS) * g_ref[...] + b_ref[...]


def _out_block(ys, lses, z, x2, w_out_bf, g, b, dils=None, tm=256):
    T, D = x2.shape
    n_groups = len(ys)
    dils = dils or (1,) * n_groups
    row = pl.BlockSpec((tm, D), lambda i: (i, 0))
    in_specs = [pl.BlockSpec((tm // d, d * D_INNER), lambda i: (i, 0)) for d in dils]
    args = list(ys)
    if n_groups > 1:
        in_specs += [pl.BlockSpec((tm // d, d * LANES), lambda i: (i, 0)) for d in dils]
        args += list(lses)
        expand = (np.arange(2 * LANES)[:, None] % LANES == (np.arange(D_INNER) // HEAD_DIM)[None, :])
        in_specs.append(pl.BlockSpec((2 * LANES, D_INNER), lambda i: (0, 0)))
        args.append(jnp.asarray(expand, dtype=BF16))
    in_specs += [row, row, pl.BlockSpec((D_INNER, D), lambda i: (0, 0)),
                 pl.BlockSpec((1, D), lambda i: (0, 0)), pl.BlockSpec((1, D), lambda i: (0, 0))]
    args += [z, x2, w_out_bf, g.reshape(1, D), b.reshape(1, D)]
    scratch = ([pltpu.VMEM((D_INNER // LANES, tm, LANES), F32), pltpu.VMEM((1, tm, LANES), F32)]
               if any(d > 1 for d in dils) else [])
    return pl.pallas_call(
        functools.partial(_out_kernel, n_groups=n_groups, dils=dils), grid=(T // tm,),
        in_specs=in_specs, out_specs=row, out_shape=jax.ShapeDtypeStruct((T, D), F32),
        scratch_shapes=scratch, compiler_params=_cparams(1), name="out_ln")(*args)


def kernel(x, sb_w_in, sb_w_out, ln0_g, ln0_b, moba_w_in, moba_w_out, ln1_g, ln1_b,
           swa_w_in, swa_sinks, swa_w_out, ln2_g, ln2_b, dil_w_in, dil_w_out, ln3_g, ln3_b):
    B, S, D = x.shape
    T = B * S
    x2 = x.reshape(T, D)
    rope = _rope_lane_tables(S)
    W = D_INNER

    segs = ((0, W, False, Q_SCALE_LOG2, 0), (W, W, False, 1.0, 1), (2 * W, W, False, 1.0, 2),
            (3 * W, W, False, 1.0, 3))
    q, k, v, z = _project(x2, sb_w_in.astype(BF16), segs, (W, W, W, W), (BF16, BF16, BF16, F32), tm=PROJ_ROWS_WIDE)
    y = _sb_attention(q.reshape(B, S, W), k.reshape(B, S, W), v.reshape(B, S, W))
    x2 = _out_block([y.reshape(T, W)], None, z, x2, sb_w_out.astype(BF16), ln0_g, ln0_b)

    segs = ((0, W, True, Q_SCALE_LOG2, 0), (W, W, True, 1.0, 1), (2 * W, W, False, 1.0, 2),
            (3 * W, W, False, 1.0, 3))
    q, k, v, z, kmean = _project(x2, moba_w_in.astype(BF16), segs, (W, W, W, W), (BF16, BF16, BF16, F32),
                                 rope_tabs=rope, kmean_seg=1, tm=PROJ_ROWS_WIDE)
    y = _moba_attention(q.reshape(B, S, W), k.reshape(B, S, W), v.reshape(B, S, W),
                        kmean.reshape(B, S // MOBA_BLOCK, W))
    x2 = _out_block([y.reshape(T, W)], None, z, x2, moba_w_out.astype(BF16), ln1_g, ln1_b)

    kvw = SWA_KV_HEADS * HEAD_DIM
    segs = ((0, W, True, Q_SCALE, 0), (W, kvw, True, 1.0, 1), (W + kvw, kvw, False, 1.0, 2),
            (W + 2 * kvw, W, False, 1.0, 3))
    q, k, v, z = _project(x2, swa_w_in.astype(BF16), segs, (W, kvw, kvw, W), (BF16, BF16, BF16, F32),
                          rope_tabs=rope, tm=PROJ_ROWS_WIDE)
    y = _band_attention(q.reshape(B, S, W), k.reshape(B, S, kvw), v.reshape(B, S, kvw), dilation=1,
                        max_back=SWA_WINDOW - 1, kv_heads=SWA_KV_HEADS, sinks=swa_sinks.astype(F32))
    x2 = _out_block([y.reshape(T, W)], None, z, x2, swa_w_out.astype(BF16), ln2_g, ln2_b)

    n_g = len(DILATED_GROUPS)
    segs = []
    for g in range(n_g):
        segs += [((3 * g) * W, W, True, Q_SCALE, 3 * g), ((3 * g + 1) * W, W, True, 1.0, 3 * g + 1),
                 ((3 * g + 2) * W, W, False, 1.0, 3 * g + 2)]
    segs.append((3 * n_g * W, W, False, 1.0, 3 * n_g))
    dils = tuple(d for _, d in DILATED_GROUPS)
    out_dils = tuple(d for d in dils for _ in range(3)) + (1,)
    outs = _project(x2, dil_w_in.astype(BF16), tuple(segs), (W,) * (3 * n_g + 1), (BF16,) * (3 * n_g) + (F32,),
                    rope_tabs=rope, out_dils=out_dils)
    z = outs[-1]
    ys, lses = [], []
    for g, (window, dil) in enumerate(DILATED_GROUPS):
        qg, kg, vg = (outs[3 * g + t].reshape(B, S // dil, dil * W) for t in range(3))
        o, lse = _band_attention(qg, kg, vg, dilation=dil, max_back=window // dil, kv_heads=N_HEADS,
                                 want_lse=True)
        ys.append(o.reshape(T // dil, dil * W))
        lses.append(lse.reshape(T // dil, dil * LANES))
    x2 = _out_block(ys, lses, z, x2, dil_w_out.astype(BF16), ln3_g, ln3_b, dils=dils)
    return x2.reshape(B, S, D)
```

```python
import functools

import jax
import jax.numpy as jnp
import numpy as np
from jax import lax
from jax.experimental import pallas as pl
from jax.experimental.pallas import tpu as pltpu

D_MODEL = 1024
HEAD_DIM = 64
N_HEADS = D_MODEL // HEAD_DIM
D_INNER = N_HEADS * HEAD_DIM
ROPE_THETA = 500000.0
ROT_DIM = HEAD_DIM // 4
LN_EPS = 1e-5
DEPTH = 4
DN_ALPHA = (2.0 * DEPTH) ** 0.25
MOBA_BLOCK = 256
MOBA_TOPK = 3
SWA_WINDOW = 128
SWA_KV_HEADS = 4
DILATED_GROUPS = ((128, 1), (512, 4), (2048, 16))
BAND_BLOCK = 128
ATTN_TILE = 512
PROJ_ROWS_WIDE = 512
SB_SUB = 256
SB_DEAD = -160.0
Q_SCALE = HEAD_DIM ** -0.5
Q_SCALE_LOG2 = Q_SCALE * float(np.log2(np.e))

LANES = 128
PAIRS = D_INNER // LANES
VMEM_LIMIT = 56 * 1024 * 1024

F32 = jnp.float32
BF16 = jnp.bfloat16
NEG_INF = float("-inf")
MASKED = -2.0 ** 60


def _cparams(n_axes):
    return pltpu.CompilerParams(dimension_semantics=("arbitrary",) * n_axes, vmem_limit_bytes=VMEM_LIMIT)


def _nt_dot(a, b):
    return lax.dot_general(a, b, (((1,), (1,)), ((), ())), preferred_element_type=F32)


def _rope_lane_tables(seq_len):
    half = ROT_DIM // 2
    pos = jnp.arange(seq_len, dtype=F32)
    inv = ROPE_THETA ** (-jnp.arange(0, ROT_DIM, 2, dtype=F32) / ROT_DIM)
    ang = pos[:, None] * inv[None, :]
    cos, sin = jnp.cos(ang), jnp.sin(ang)
    hl = np.arange(LANES) % HEAD_DIM
    idx = jnp.asarray(hl % half)
    cos_l, sin_l = cos[:, idx], sin[:, idx]
    c = jnp.where(jnp.asarray(hl < ROT_DIM)[None, :], cos_l, 1.0)
    s1 = jnp.where(jnp.asarray(hl < half)[None, :], -sin_l, 0.0)
    s2 = jnp.where(jnp.asarray((hl >= half) & (hl < ROT_DIM))[None, :], sin_l, 0.0)
    return c.astype(F32), s1.astype(F32), s2.astype(F32)


def _apply_rope(y, c, s1, s2):
    parts = []
    for t in range(y.shape[1] // LANES):
        yt = y[:, t * LANES:(t + 1) * LANES]
        up = pltpu.roll(yt, LANES - ROT_DIM // 2, 1)
        dn = pltpu.roll(yt, ROT_DIM // 2, 1)
        parts.append(yt * c + up * s1 + dn * s2)
    return parts[0] if len(parts) == 1 else jnp.concatenate(parts, axis=1)


def _proj_kernel(*refs, segs, use_rope, kmean_seg, out_dils):
    x_ref, w_ref = refs[0], refs[1]
    pos = 2
    if use_rope:
        c, s1, s2 = refs[2][...], refs[3][...], refs[4][...]
        pos = 5
    streamed = any(d > 1 for d in out_dils)
    outs = refs[pos:-1] if streamed else refs[pos:]
    xb = x_ref[...].astype(BF16)
    for si, (col0, width, rope, scale, oi) in enumerate(segs):
        y = jnp.dot(xb, w_ref[:, col0:col0 + width], preferred_element_type=F32)
        if rope:
            y = _apply_rope(y, c, s1, s2)
        if scale != 1.0:
            y = y * scale
        d = out_dils[oi]
        if d == 1:
            outs[oi][...] = y.astype(outs[oi].dtype)
        else:
            y_scr = refs[-1]
            rows = y.shape[0] // d
            for t in range(width // LANES):
                y_scr[t] = y[:, t * LANES:(t + 1) * LANES]
            for r in range(d):
                for t in range(width // LANES):
                    lanes = slice(r * width + t * LANES, r * width + (t + 1) * LANES)
                    outs[oi][:, lanes] = y_scr[t, pl.ds(r, rows, stride=d), :].astype(outs[oi].dtype)
        if kmean_seg == si:
            tm = y.shape[0]
            km = outs[-1]
            for blk in range(tm // MOBA_BLOCK):
                rows = y[blk * MOBA_BLOCK:(blk + 1) * MOBA_BLOCK, :]
                km[0, blk:blk + 1, :] = jnp.sum(rows, axis=0, keepdims=True) * (1.0 / MOBA_BLOCK)


def _project(x2, w_bf, segs, out_widths, out_dtypes, rope_tabs=None, kmean_seg=None, out_dils=None, tm=256):
    T, D = x2.shape
    N = w_bf.shape[1]
    out_dils = out_dils or (1,) * len(out_widths)
    seq = rope_tabs[0].shape[0] if rope_tabs is not None else None
    in_specs = [pl.BlockSpec((tm, D), lambda i: (i, 0)),
                pl.BlockSpec((D, N), lambda i: (0, 0), pipeline_mode=pl.Buffered(1))]
    args = [x2, w_bf]
    if rope_tabs is not None:
        nblk = seq // tm
        for t in rope_tabs:
            in_specs.append(pl.BlockSpec((tm, LANES), lambda i: (i % nblk, 0)))
            args.append(t)
    out_shape = [jax.ShapeDtypeStruct((T // d, d * w), dt) for w, dt, d in zip(out_widths, out_dtypes, out_dils)]
    out_specs = [pl.BlockSpec((tm // d, d * w), lambda i: (i, 0)) for w, d in zip(out_widths, out_dils)]
    if kmean_seg is not None:
        nb = tm // MOBA_BLOCK
        out_shape.append(jax.ShapeDtypeStruct((T // tm, nb, D_INNER), F32))
        out_specs.append(pl.BlockSpec((1, nb, D_INNER), lambda i: (i, 0, 0)))
    scratch = [pltpu.VMEM((max(out_widths) // LANES, tm, LANES), F32)] if any(d > 1 for d in out_dils) else []
    kern = functools.partial(_proj_kernel, segs=segs, use_rope=rope_tabs is not None, kmean_seg=kmean_seg,
                             out_dils=out_dils)
    return pl.pallas_call(
        kern, grid=(T // tm,), in_specs=in_specs, out_specs=out_specs, out_shape=out_shape,
        scratch_shapes=scratch, compiler_params=_cparams(1), name="in_proj")(*args)


def _sb_kernel(q_ref, k_ref, v_ref, o_ref, g_scr, *, tq):
    i = pl.program_id(2)
    q = q_ref[0]
    lane = lax.broadcasted_iota(jnp.int32, (tq, LANES), 1)
    zero = jnp.zeros_like(q)
    q2 = jnp.concatenate([jnp.where(lane < HEAD_DIM, q, zero), jnp.where(lane >= HEAD_DIM, q, zero)], axis=0)
    sub = SB_SUB
    row = lax.broadcasted_iota(jnp.int32, (sub, sub), 0)
    col = lax.broadcasted_iota(jnp.int32, (sub, sub), 1)
    suffix = jnp.where(row > col, 1.0, 0.0).astype(BF16)

    n_sub = tq // sub

    def weigh(j, slot, diag):
        start = pl.multiple_of(j * tq, tq)
        z = _nt_dot(q2, k_ref[0, pl.ds(start, tq), :])
        neg_abs = lax.bitcast_convert_type(lax.bitcast_convert_type(z, jnp.int32) | jnp.int32(-2 ** 31), F32)
        log_beta = jnp.minimum(z, 0.0) - jnp.log2(1.0 + jnp.exp2(neg_abs))
        l1m = log_beta - z
        if diag:
            qrow = lax.broadcasted_iota(jnp.int32, (2 * tq, tq), 0) & (tq - 1)
            before = lax.broadcasted_iota(jnp.int32, (2 * tq, tq), 1) < qrow
            l1m = jnp.where(before, l1m, 0.0)
            log_beta = jnp.where(before, log_beta, NEG_INF)
        lb = l1m.astype(BF16)
        sums = []
        for s in range(n_sub):
            blk = slice(s * sub, (s + 1) * sub)
            g_scr[slot, :, blk] = log_beta[:, blk] + jnp.dot(lb[:, blk], suffix, preferred_element_type=F32)
            sums.append(jnp.sum(l1m[:, blk], axis=1, keepdims=True))
        return tuple(sums)

    def gather(j, slot, sums, c, acc):
        start = pl.multiple_of(j * tq, tq)
        parts = [None] * n_sub
        for s in reversed(range(n_sub)):
            parts[s] = jnp.exp2(g_scr[slot, :, s * sub:(s + 1) * sub] + c).astype(BF16)
            c = c + sums[s]
        a = jnp.concatenate(parts, axis=1)
        acc = acc + jnp.dot(a, v_ref[0, pl.ds(start, tq), :], preferred_element_type=F32)
        return c, acc

    odd = i % 2
    sums = weigh(i, odd, True)
    carry = (sums, jnp.zeros((2 * tq, 1), F32), jnp.zeros((2 * tq, LANES), F32))

    def single(n, cr):
        c1, a1 = gather(i, 1, cr[0], cr[1], cr[2])
        return weigh(i - 1, 0, False), c1, a1

    def double(n, cr):
        j = i - odd - 2 * n
        c1, a1 = gather(j, 0, cr[0], cr[1], cr[2])
        s1 = weigh(j - 1, 1, False)
        c2, a2 = gather(j - 1, 1, s1, c1, a1)
        return weigh(j - 2, 0, False), c2, a2

    carry = lax.fori_loop(0, odd, single, carry)
    sums, c, acc = lax.fori_loop(0, (i - odd) // 2, double, carry)
    _, acc = gather(0, 0, sums, c, acc)
    o_ref[0] = jnp.where(lane < HEAD_DIM, acc[:tq], acc[tq:])


def _sb_kernel_t(q_ref, k_ref, v_ref, o_ref, g_scr, *, tq):
    i = pl.program_id(2)
    dim = lax.broadcasted_iota(jnp.int32, (LANES, tq), 0)
    q_t = q_ref[0].astype(F32).T
    zero = jnp.zeros_like(q_t)
    q2_t = jnp.concatenate([jnp.where(dim < HEAD_DIM, q_t, zero), jnp.where(dim >= HEAD_DIM, q_t, zero)],
                           axis=1).astype(BF16)
    sub = SB_SUB
    n_sub = tq // sub
    r = lax.broadcasted_iota(jnp.int32, (sub + 8, sub), 0)
    c_ = lax.broadcasted_iota(jnp.int32, (sub + 8, sub), 1)
    sfx = jnp.where((c_ > r) | (r >= sub), 1.0, 0.0).astype(BF16)

    def weigh(j, slot, diag):
        start = pl.multiple_of(j * tq, tq)
        z = jnp.dot(k_ref[0, pl.ds(start, tq), :], q2_t, preferred_element_type=F32)
        neg_abs = lax.bitcast_convert_type(lax.bitcast_convert_type(z, jnp.int32) | jnp.int32(-2 ** 31), F32)
        log_beta = jnp.minimum(z, 0.0) - jnp.log2(1.0 + jnp.exp2(neg_abs))
        l1m = log_beta - z
        if diag:
            before = lax.broadcasted_iota(jnp.int32, (tq, 2 * tq), 0) < (
                lax.broadcasted_iota(jnp.int32, (tq, 2 * tq), 1) & (tq - 1))
            l1m = jnp.where(before, l1m, 0.0)
            log_beta = jnp.where(before, log_beta, NEG_INF)
        lb = l1m.astype(BF16)
        sums = []
        for s in range(n_sub):
            rows = slice(s * sub, (s + 1) * sub)
            ext = jnp.dot(sfx, lb[rows], preferred_element_type=F32)
            g_scr[slot, rows, :] = log_beta[rows] + ext[:sub]
            sums.append(ext[sub:sub + 1])
        return tuple(sums)

    def gather(j, slot, sums, c, acc):
        start = pl.multiple_of(j * tq, tq)
        parts = [None] * n_sub
        for s in reversed(range(n_sub)):
            parts[s] = jnp.exp2(g_scr[slot, s * sub:(s + 1) * sub, :] + c).astype(BF16)
            c = c + sums[s]
        a = jnp.concatenate(parts, axis=0)
        v_t = v_ref[0, pl.ds(start, tq), :].astype(F32).T.astype(BF16)
        acc = acc + jnp.dot(v_t, a, preferred_element_type=F32)
        return c, acc

    odd = i % 2
    sums = weigh(i, odd, True)
    carry = (sums, jnp.zeros((1, 2 * tq), F32), jnp.zeros((LANES, 2 * tq), F32))

    def single(n, cr):
        c1, a1 = gather(i, 1, cr[0], cr[1], cr[2])
        return weigh(i - 1, 0, False), c1, a1

    def alive(c):
        return (jnp.max(c) > SB_DEAD).astype(jnp.int32)

    def carry_after(sums, c):
        for s in reversed(range(n_sub)):
            c = c + sums[s]
        return c

    def double(cr):
        n, _, sums, c, acc = cr
        j = i - odd - 2 * n

        def with_next(_):
            c1, a1 = gather(j, 0, sums, c, acc)
            s1 = weigh(j - 1, 1, False)

            def and_after(_):
                c2, a2 = gather(j - 1, 1, s1, c1, a1)
                return jnp.int32(1), weigh(j - 2, 0, False), c2, a2

            def then_stop(_):
                c2, a2 = gather(j - 1, 1, s1, c1, a1)
                return jnp.int32(0), s1, c2, a2

            return lax.cond(alive(carry_after(s1, c1)) > 0, and_after, then_stop, None)

        def only_this(_):
            c1, a1 = gather(j, 0, sums, c, acc)
            return jnp.int32(0), sums, c1, a1

        live, s2, c2, a2 = lax.cond(alive(carry_after(sums, c)) > 0, with_next, only_this, None)
        return n + 1, live, s2, c2, a2

    n_trips = (i - odd) // 2
    sums, c, acc = lax.fori_loop(0, odd, single, carry)
    n, live, sums, c, acc = lax.while_loop(lambda cr: (cr[0] < n_trips) & (cr[1] > 0), double,
                                           (jnp.int32(0), alive(c), sums, c, acc))
    acc = lax.cond(live > 0, lambda _: gather(i - odd - 2 * n, 0, sums, c, acc)[1], lambda _: acc, None)
    o_ref[0] = jnp.where(dim < HEAD_DIM, acc[:, :tq], acc[:, tq:]).T


def _sb_attention(q, k, v, tq=ATTN_TILE):
    B, S, _ = q.shape
    grid = (B, PAIRS, S // tq)
    return pl.pallas_call(
        functools.partial(_sb_kernel_t, tq=tq), grid=grid,
        in_specs=[pl.BlockSpec((1, tq, LANES), lambda b, p, i: (b, i, p)),
                  pl.BlockSpec((1, S, LANES), lambda b, p, i: (b, 0, p)),
                  pl.BlockSpec((1, S, LANES), lambda b, p, i: (b, 0, p))],
        out_specs=pl.BlockSpec((1, tq, LANES), lambda b, p, i: (b, i, p)),
        out_shape=jax.ShapeDtypeStruct((B, S, D_INNER), F32),
        scratch_shapes=[pltpu.VMEM((2, tq, 2 * tq), F32)],
        compiler_params=_cparams(3), name="sb_attn")(q, k, v)


def _moba_kernel_t(q_ref, k_ref, v_ref, km_ref, o_ref, s_scr, *, nkb, tq):
    i = pl.program_id(2)
    km = km_ref[0]
    per_tile = tq // MOBA_BLOCK
    dim = lax.broadcasted_iota(jnp.int32, (LANES, tq), 0)
    q_t = q_ref[0].astype(F32).T
    zero = jnp.zeros_like(q_t)
    q2_t = jnp.concatenate([jnp.where(dim < HEAD_DIM, q_t, zero), jnp.where(dim >= HEAD_DIM, q_t, zero)],
                           axis=1).astype(BF16)

    km_hi = km.astype(BF16)
    km_lo = (km - km_hi.astype(F32)).astype(BF16)
    gate = jnp.dot(jnp.concatenate([km_hi, km_lo], axis=1), jnp.concatenate([q2_t, q2_t], axis=0),
                   preferred_element_type=F32)
    blk = lax.broadcasted_iota(jnp.int32, (nkb, 2 * tq), 0)
    blk_f = blk.astype(F32)
    qcol = lax.broadcasted_iota(jnp.int32, (nkb, 2 * tq), 1) & (tq - 1)
    q_blk = i * per_tile + qcol // MOBA_BLOCK
    past = blk < q_blk
    g = jnp.where(past, gate, NEG_INF)
    sel = jnp.zeros((nkb, 2 * tq), jnp.bool_)
    for _ in range(MOBA_TOPK):
        mx = jnp.max(g, axis=0, keepdims=True)
        first = jnp.min(jnp.where(g == mx, blk_f, float(nkb)), axis=0, keepdims=True)
        pick = blk_f == first
        sel = sel | (pick & past)
        g = jnp.where(pick, NEG_INF, g)
    bias = jnp.where(sel | (blk == q_blk), 0.0, MASKED).astype(BF16)
    qx_t = jnp.concatenate([q2_t, bias, jnp.zeros((LANES - nkb, 2 * tq), BF16)], axis=0)
    lane_k = lax.broadcasted_iota(jnp.int32, (tq, LANES), 1)
    key_blk = lax.broadcasted_iota(jnp.int32, (tq, LANES), 0) // MOBA_BLOCK

    def score(j, slot, diag):
        st = pl.multiple_of(j * tq, tq)
        onehot = jnp.where(lane_k == j * per_tile + key_blk, 1.0, 0.0).astype(BF16)
        kx = jnp.concatenate([k_ref[0, pl.ds(st, tq), :], onehot], axis=1)
        s = jnp.dot(kx, qx_t, preferred_element_type=F32)
        if diag:
            causal = lax.broadcasted_iota(jnp.int32, (tq, 2 * tq), 0) <= (
                lax.broadcasted_iota(jnp.int32, (tq, 2 * tq), 1) & (tq - 1))
            s = jnp.where(causal, s, NEG_INF)
        s_scr[slot] = s
        return jnp.max(s, axis=0, keepdims=True)

    def absorb(j, slot, m_tile, carry):
        m, l, acc = carry
        st = pl.multiple_of(j * tq, tq)
        v_t = v_ref[0, pl.ds(st, tq), :].astype(F32).T.astype(BF16)
        m_new = jnp.maximum(m, m_tile)
        alpha = jnp.exp2(m - m_new)
        pb = jnp.exp2(s_scr[slot] - m_new).astype(BF16)
        ones = jnp.ones((16, tq), BF16)
        res = [jnp.dot(jnp.concatenate([v_t[h * HEAD_DIM:(h + 1) * HEAD_DIM], ones], axis=0),
                       pb[:, h * tq:(h + 1) * tq], preferred_element_type=F32) for h in range(2)]
        l = alpha * l + jnp.concatenate([r_[HEAD_DIM:HEAD_DIM + 1] for r_ in res], axis=1)
        acc = alpha * acc + jnp.concatenate([r_[:HEAD_DIM] for r_ in res], axis=1)
        return m_new, l, acc

    odd = i % 2
    m_tile = score(i, odd, True)
    state = (jnp.full((1, 2 * tq), NEG_INF, F32), jnp.zeros((1, 2 * tq), F32), jnp.zeros((HEAD_DIM, 2 * tq), F32))

    def single(n, cr):
        st1 = absorb(i, 1, cr[0], cr[1])
        return score(i - 1, 0, False), st1

    def double(n, cr):
        j = i - odd - 2 * n
        st1 = absorb(j, 0, cr[0], cr[1])
        m1 = score(j - 1, 1, False)
        st2 = absorb(j - 1, 1, m1, st1)
        return score(j - 2, 0, False), st2

    carry = lax.fori_loop(0, odd, single, (m_tile, state))
    m_tile, state = lax.fori_loop(0, (i - odd) // 2, double, carry)
    _, l, acc = absorb(0, 0, m_tile, state)
    out = acc / l
    o_ref[0] = jnp.concatenate([out[:, :tq], out[:, tq:]], axis=0).T


def _moba_attention(q, k, v, kmean, tq=ATTN_TILE):
    B, S, _ = q.shape
    nkb = S // MOBA_BLOCK
    grid = (B, PAIRS, S // tq)
    return pl.pallas_call(
        functools.partial(_moba_kernel_t, nkb=nkb, tq=tq), grid=grid,
        in_specs=[pl.BlockSpec((1, tq, LANES), lambda b, p, i: (b, i, p)),
                  pl.BlockSpec((1, S, LANES), lambda b, p, i: (b, 0, p)),
                  pl.BlockSpec((1, S, LANES), lambda b, p, i: (b, 0, p)),
                  pl.BlockSpec((1, nkb, LANES), lambda b, p, i: (b, 0, p))],
        out_specs=pl.BlockSpec((1, tq, LANES), lambda b, p, i: (b, i, p)),
        out_shape=jax.ShapeDtypeStruct((B, S, D_INNER), F32),
        scratch_shapes=[pltpu.VMEM((2, tq, 2 * tq), F32)],
        compiler_params=_cparams(3), name="moba_attn")(q, k, v, kmean)


def _band_kernel(*refs, max_back, kv_heads, use_sinks, want_lse):
    tq = BAND_BLOCK
    pos = 0
    if use_sinks:
        sink_ref = refs[0]
        pos = 1
    q_ref, kp_ref, kc_ref, vp_ref, vc_ref = refs[pos:pos + 5]
    o_ref = refs[pos + 5]
    lse_ref = refs[pos + 6] if want_lse else None
    i = pl.program_id(2)
    rep = N_HEADS // kv_heads

    lane = lax.broadcasted_iota(jnp.int32, (tq, LANES), 1)
    row = lax.broadcasted_iota(jnp.int32, (tq, 2 * tq), 0)
    col = lax.broadcasted_iota(jnp.int32, (tq, 2 * tq), 1)
    dist = row - col + tq
    first_key = jnp.where(i > 0, 0, tq)
    valid = (dist >= 0) & (dist <= max_back) & (col >= first_key)
    low_half = lane < HEAD_DIM

    kcat = jnp.concatenate([kp_ref[0], kc_ref[0]], axis=0)
    vcat = jnp.concatenate([vp_ref[0], vc_ref[0]], axis=0)
    lse_acc = jnp.zeros((tq, LANES), F32)
    for pr in range(PAIRS):
        qf = q_ref[0, :, pr * LANES:(pr + 1) * LANES].astype(F32)
        q_same = qf.astype(BF16)
        q_swap = pltpu.roll(qf, HEAD_DIM, 1).astype(BF16)
        outs = []
        for hh in range(2):
            h = 2 * pr + hh
            g = h // rep
            gh = g % 2
            qsrc = q_same if gh == hh else q_swap
            qm = jnp.where(low_half == (gh == 0), qsrc, jnp.zeros_like(qsrc))
            kt = kcat[:, (g // 2) * LANES:(g // 2 + 1) * LANES]
            vt = vcat[:, (g // 2) * LANES:(g // 2 + 1) * LANES]
            s = jnp.where(valid, _nt_dot(qm, kt), NEG_INF)
            m = jnp.max(s, axis=1, keepdims=True)
            if use_sinks:
                sink = sink_ref[h]
                m = jnp.maximum(m, sink)
            e = jnp.exp(s - m)
            den = jnp.sum(e, axis=1, keepdims=True)
            if use_sinks:
                den = den + jnp.exp(sink - m)
            o = jnp.dot(e.astype(BF16), vt, preferred_element_type=F32) / den
            if gh != hh:
                o = pltpu.roll(o, HEAD_DIM, 1)
            outs.append(o)
            if want_lse:
                lse_acc = jnp.where(lane == h, m + jnp.log(den), lse_acc)
        o_ref[0, :, pr * LANES:(pr + 1) * LANES] = jnp.where(low_half, outs[0], outs[1])
    if want_lse:
        lse_ref[0] = lse_acc


def _band_attention(q, k, v, *, dilation, max_back, kv_heads, sinks=None, want_lse=False):
    B, L, _ = q.shape
    kvw = kv_heads * HEAD_DIM
    tq = BAND_BLOCK
    qv, kv_, vv = q, k, v
    grid = (B, dilation, L // tq)
    cur = lambda b, r, i: (b, i, r)
    prev = lambda b, r, i: (b, jnp.maximum(i - 1, 0), r)
    in_specs = [pl.BlockSpec((1, tq, D_INNER), cur),
                pl.BlockSpec((1, tq, kvw), prev), pl.BlockSpec((1, tq, kvw), cur),
                pl.BlockSpec((1, tq, kvw), prev), pl.BlockSpec((1, tq, kvw), cur)]
    args = [qv, kv_, kv_, vv, vv]
    if sinks is not None:
        in_specs = [pl.BlockSpec(memory_space=pltpu.SMEM)] + in_specs
        args = [sinks] + args
    out_shape = [jax.ShapeDtypeStruct((B, L, dilation * D_INNER), F32)]
    out_specs = [pl.BlockSpec((1, tq, D_INNER), cur)]
    if want_lse:
        out_shape.append(jax.ShapeDtypeStruct((B, L, dilation * LANES), F32))
        out_specs.append(pl.BlockSpec((1, tq, LANES), cur))
    kern = functools.partial(_band_kernel, max_back=max_back, kv_heads=kv_heads,
                             use_sinks=sinks is not None, want_lse=want_lse)
    res = pl.pallas_call(kern, grid=grid, in_specs=in_specs, out_specs=out_specs, out_shape=out_shape,
                         compiler_params=_cparams(3), name="band_attn")(*args)
    return tuple(res) if want_lse else res[0]


def _out_kernel(*refs, n_groups, dils):
    ys = refs[:n_groups]
    pos = n_groups
    if n_groups > 1:
        lses = refs[pos:pos + n_groups]
        expand_ref = refs[pos + n_groups]
        pos += n_groups + 1
    z_ref, x_ref, w_ref, g_ref, b_ref, o_ref = refs[pos:pos + 6]
    y_scr, l_scr = refs[pos + 6:pos + 8] if any(d > 1 for d in dils) else (None, None)

    def token_order(ref, scr, d, width):
        if d == 1:
            return ref[...]
        rows = ref.shape[0]
        tiles = width // LANES
        for r in range(d):
            for t in range(tiles):
                scr[t, pl.ds(r, rows, stride=d), :] = ref[:, r * width + t * LANES:r * width + (t + 1) * LANES]
        return jnp.concatenate([scr[t] for t in range(tiles)], axis=1) if tiles > 1 else scr[0]

    if n_groups == 1:
        y = ys[0][...]
    else:
        ls = [token_order(r, l_scr, d, LANES) for r, d in zip(lses, dils)]
        mx = functools.reduce(jnp.maximum, ls)
        es = [jnp.exp(l - mx) for l in ls]
        tot = functools.reduce(lambda a, b: a + b, es)
        y = None
        for e, yr, d in zip(es, ys, dils):
            w = e / tot
            w_hi = w.astype(BF16)
            w_lo = (w - w_hi.astype(F32)).astype(BF16)
            wts = jnp.dot(jnp.concatenate([w_hi, w_lo], axis=1), expand_ref[...], preferred_element_type=F32)
            yg = wts * token_order(yr, y_scr, d, D_INNER)
            y = yg if y is None else y + yg
    z = z_ref[...]
    u = (y * (z * jax.nn.sigmoid(z))).astype(BF16)
    r = DN_ALPHA * x_ref[...] + jnp.dot(u, w_ref[...], preferred_element_type=F32)
    mu = jnp.mean(r, axis=1, keepdims=True)
    d = r - mu
    var = jnp.mean(d * d, axis=1, keepdims=True)
    o_ref[...] = d * lax.rsqrt(var + LN_EPS) * g_ref[...] + b_ref[...]


def _out_block(ys, lses, z, x2, w_out_bf, g, b, dils=None, tm=256):
    T, D = x2.shape
    n_groups = len(ys)
    dils = dils or (1,) * n_groups
    row = pl.BlockSpec((tm, D), lambda i: (i, 0))
    in_specs = [pl.BlockSpec((tm // d, d * D_INNER), lambda i: (i, 0)) for d in dils]
    args = list(ys)
    if n_groups > 1:
        in_specs += [pl.BlockSpec((tm // d, d * LANES), lambda i: (i, 0)) for d in dils]
        args += list(lses)
        expand = (np.arange(2 * LANES)[:, None] % LANES == (np.arange(D_INNER) // HEAD_DIM)[None, :])
        in_specs.append(pl.BlockSpec((2 * LANES, D_INNER), lambda i: (0, 0)))
        args.append(jnp.asarray(expand, dtype=BF16))
    in_specs += [row, row, pl.BlockSpec((D_INNER, D), lambda i: (0, 0)),
                 pl.BlockSpec((1, D), lambda i: (0, 0)), pl.BlockSpec((1, D), lambda i: (0, 0))]
    args += [z, x2, w_out_bf, g.reshape(1, D), b.reshape(1, D)]
    scratch = ([pltpu.VMEM((D_INNER // LANES, tm, LANES), F32), pltpu.VMEM((1, tm, LANES), F32)]
               if any(d > 1 for d in dils) else [])
    return pl.pallas_call(
        functools.partial(_out_kernel, n_groups=n_groups, dils=dils), grid=(T // tm,),
        in_specs=in_specs, out_specs=row, out_shape=jax.ShapeDtypeStruct((T, D), F32),
        scratch_shapes=scratch, compiler_params=_cparams(1), name="out_ln")(*args)


def kernel(x, sb_w_in, sb_w_out, ln0_g, ln0_b, moba_w_in, moba_w_out, ln1_g, ln1_b,
           swa_w_in, swa_sinks, swa_w_out, ln2_g, ln2_b, dil_w_in, dil_w_out, ln3_g, ln3_b):
    B, S, D = x.shape
    T = B * S
    x2 = x.reshape(T, D)
    rope = _rope_lane_tables(S)
    W = D_INNER

    segs = ((0, W, False, Q_SCALE_LOG2, 0), (W, W, False, 1.0, 1), (2 * W, W, False, 1.0, 2),
            (3 * W, W, False, 1.0, 3))
    q, k, v, z = _project(x2, sb_w_in.astype(BF16), segs, (W, W, W, W), (BF16, BF16, BF16, F32), tm=PROJ_ROWS_WIDE)
    y = _sb_attention(q.reshape(B, S, W), k.reshape(B, S, W), v.reshape(B, S, W))
    x2 = _out_block([y.reshape(T, W)], None, z, x2, sb_w_out.astype(BF16), ln0_g, ln0_b)

    segs = ((0, W, True, Q_SCALE_LOG2, 0), (W, W, True, 1.0, 1), (2 * W, W, False, 1.0, 2),
            (3 * W, W, False, 1.0, 3))
    q, k, v, z, kmean = _project(x2, moba_w_in.astype(BF16), segs, (W, W, W, W), (BF16, BF16, BF16, F32),
                                 rope_tabs=rope, kmean_seg=1, tm=PROJ_ROWS_WIDE)
    y = _moba_attention(q.reshape(B, S, W), k.reshape(B, S, W), v.reshape(B, S, W),
                        kmean.reshape(B, S // MOBA_BLOCK, W))
    x2 = _out_block([y.reshape(T, W)], None, z, x2, moba_w_out.astype(BF16), ln1_g, ln1_b)

    kvw = SWA_KV_HEADS * HEAD_DIM
    segs = ((0, W, True, Q_SCALE, 0), (W, kvw, True, 1.0, 1), (W + kvw, kvw, False, 1.0, 2),
            (W + 2 * kvw, W, False, 1.0, 3))
    q, k, v, z = _project(x2, swa_w_in.astype(BF16), segs, (W, kvw, kvw, W), (BF16, BF16, BF16, F32),
                          rope_tabs=rope, tm=PROJ_ROWS_WIDE)
    y = _band_attention(q.reshape(B, S, W), k.reshape(B, S, kvw), v.reshape(B, S, kvw), dilation=1,
                        max_back=SWA_WINDOW - 1, kv_heads=SWA_KV_HEADS, sinks=swa_sinks.astype(F32))
    x2 = _out_block([y.reshape(T, W)], None, z, x2, swa_w_out.astype(BF16), ln2_g, ln2_b)

    n_g = len(DILATED_GROUPS)
    segs = []
    for g in range(n_g):
        segs += [((3 * g) * W, W, True, Q_SCALE, 3 * g), ((3 * g + 1) * W, W, True, 1.0, 3 * g + 1),
                 ((3 * g + 2) * W, W, False, 1.0, 3 * g + 2)]
    segs.append((3 * n_g * W, W, False, 1.0, 3 * n_g))
    dils = tuple(d for _, d in DILATED_GROUPS)
    out_dils = tuple(d for d in dils for _ in range(3)) + (1,)
    outs = _project(x2, dil_w_in.astype(BF16), tuple(segs), (W,) * (3 * n_g + 1), (BF16,) * (3 * n_g) + (F32,),
                    rope_tabs=rope, out_dils=out_dils)
    z = outs[-1]
    ys, lses = [], []
    for g, (window, dil) in enumerate(DILATED_GROUPS):
        qg, kg, vg = (outs[3 * g + t].reshape(B, S // dil, dil * W) for t in range(3))
        o, lse = _band_attention(qg, kg, vg, dilation=dil, max_back=window // dil, kv_heads=N_HEADS,
                                 want_lse=True)
        ys.append(o.reshape(T // dil, dil * W))
        lses.append(lse.reshape(T // dil, dil * LANES))
    x2 = _out_block(ys, lses, z, x2, dil_w_out.astype(BF16), ln3_g, ln3_b, dils=dils)
    return x2.reshape(B, S, D)
```

```python
import functools

import jax
import jax.numpy as jnp
import numpy as np
from jax import lax
from jax.experimental import pallas as pl
from jax.experimental.pallas import tpu as pltpu

D_MODEL = 1024
HEAD_DIM = 64
N_HEADS = D_MODEL // HEAD_DIM
D_INNER = N_HEADS * HEAD_DIM
ROPE_THETA = 500000.0
ROT_DIM = HEAD_DIM // 4
LN_EPS = 1e-5
DEPTH = 4
DN_ALPHA = (2.0 * DEPTH) ** 0.25
MOBA_BLOCK = 256
MOBA_TOPK = 3
SWA_WINDOW = 128
SWA_KV_HEADS = 4
DILATED_GROUPS = ((128, 1), (512, 4), (2048, 16))
BAND_BLOCK = 128
ATTN_TILE = 512
PROJ_ROWS_WIDE = 512
SB_SUB = 256
SB_DEAD = -160.0
Q_SCALE = HEAD_DIM ** -0.5
Q_SCALE_LOG2 = Q_SCALE * float(np.log2(np.e))

LANES = 128
PAIRS = D_INNER // LANES
VMEM_LIMIT = 56 * 1024 * 1024

F32 = jnp.float32
BF16 = jnp.bfloat16
NEG_INF = float("-inf")
MASKED = -2.0 ** 60


def _cparams(n_axes):
    return pltpu.CompilerParams(dimension_semantics=("arbitrary",) * n_axes, vmem_limit_bytes=VMEM_LIMIT)


def _nt_dot(a, b):
    return lax.dot_general(a, b, (((1,), (1,)), ((), ())), preferred_element_type=F32)


def _rope_lane_tables(seq_len):
    half = ROT_DIM // 2
    pos = jnp.arange(seq_len, dtype=F32)
    inv = ROPE_THETA ** (-jnp.arange(0, ROT_DIM, 2, dtype=F32) / ROT_DIM)
    ang = pos[:, None] * inv[None, :]
    cos, sin = jnp.cos(ang), jnp.sin(ang)
    hl = np.arange(LANES) % HEAD_DIM
    idx = jnp.asarray(hl % half)
    cos_l, sin_l = cos[:, idx], sin[:, idx]
    c = jnp.where(jnp.asarray(hl < ROT_DIM)[None, :], cos_l, 1.0)
    s1 = jnp.where(jnp.asarray(hl < half)[None, :], -sin_l, 0.0)
    s2 = jnp.where(jnp.asarray((hl >= half) & (hl < ROT_DIM))[None, :], sin_l, 0.0)
    return c.astype(F32), s1.astype(F32), s2.astype(F32)


def _apply_rope(y, c, s1, s2):
    parts = []
    for t in range(y.shape[1] // LANES):
        yt = y[:, t * LANES:(t + 1) * LANES]
        up = pltpu.roll(yt, LANES - ROT_DIM // 2, 1)
        dn = pltpu.roll(yt, ROT_DIM // 2, 1)
        parts.append(yt * c + up * s1 + dn * s2)
    return parts[0] if len(parts) == 1 else jnp.concatenate(parts, axis=1)


def _proj_kernel(*refs, segs, use_rope, kmean_seg, out_dils):
    x_ref, w_ref = refs[0], refs[1]
    pos = 2
    if use_rope:
        c, s1, s2 = refs[2][...], refs[3][...], refs[4][...]
        pos = 5
    streamed = any(d > 1 for d in out_dils)
    outs = refs[pos:-1] if streamed else refs[pos:]
    xb = x_ref[...].astype(BF16)
    for si, (col0, width, rope, scale, oi) in enumerate(segs):
        y = jnp.dot(xb, w_ref[:, col0:col0 + width], preferred_element_type=F32)
        if rope:
            y = _apply_rope(y, c, s1, s2)
        if scale != 1.0:
            y = y * scale
        d = out_dils[oi]
        if d == 1:
            outs[oi][...] = y.astype(outs[oi].dtype)
        else:
            y_scr = refs[-1]
            rows = y.shape[0] // d
            for t in range(width // LANES):
                y_scr[t] = y[:, t * LANES:(t + 1) * LANES]
            for r in range(d):
                for t in range(width // LANES):
                    lanes = slice(r * width + t * LANES, r * width + (t + 1) * LANES)
                    outs[oi][:, lanes] = y_scr[t, pl.ds(r, rows, stride=d), :].astype(outs[oi].dtype)
        if kmean_seg == si:
            tm = y.shape[0]
            km = outs[-1]
            for blk in range(tm // MOBA_BLOCK):
                rows = y[blk * MOBA_BLOCK:(blk + 1) * MOBA_BLOCK, :]
                km[0, blk:blk + 1, :] = jnp.sum(rows, axis=0, keepdims=True) * (1.0 / MOBA_BLOCK)


def _project(x2, w_bf, segs, out_widths, out_dtypes, rope_tabs=None, kmean_seg=None, out_dils=None, tm=256):
    T, D = x2.shape
    N = w_bf.shape[1]
    out_dils = out_dils or (1,) * len(out_widths)
    seq = rope_tabs[0].shape[0] if rope_tabs is not None else None
    in_specs = [pl.BlockSpec((tm, D), lambda i: (i, 0)),
                pl.BlockSpec((D, N), lambda i: (0, 0), pipeline_mode=pl.Buffered(1))]
    args = [x2, w_bf]
    if rope_tabs is not None:
        nblk = seq // tm
        for t in rope_tabs:
            in_specs.append(pl.BlockSpec((tm, LANES), lambda i: (i % nblk, 0)))
            args.append(t)
    out_shape = [jax.ShapeDtypeStruct((T // d, d * w), dt) for w, dt, d in zip(out_widths, out_dtypes, out_dils)]
    out_specs = [pl.BlockSpec((tm // d, d * w), lambda i: (i, 0)) for w, d in zip(out_widths, out_dils)]
    if kmean_seg is not None:
        nb = tm // MOBA_BLOCK
        out_shape.append(jax.ShapeDtypeStruct((T // tm, nb, D_INNER), F32))
        out_specs.append(pl.BlockSpec((1, nb, D_INNER), lambda i: (i, 0, 0)))
    scratch = [pltpu.VMEM((max(out_widths) // LANES, tm, LANES), F32)] if any(d > 1 for d in out_dils) else []
    kern = functools.partial(_proj_kernel, segs=segs, use_rope=rope_tabs is not None, kmean_seg=kmean_seg,
                             out_dils=out_dils)
    return pl.pallas_call(
        kern, grid=(T // tm,), in_specs=in_specs, out_specs=out_specs, out_shape=out_shape,
        scratch_shapes=scratch, compiler_params=_cparams(1), name="in_proj")(*args)


def _sb_kernel_t(q_ref, k_ref, v_ref, o_ref, g_scr, *, tq):
    i = pl.program_id(2)
    dim = lax.broadcasted_iota(jnp.int32, (LANES, tq), 0)
    q_t = q_ref[0].astype(F32).T
    zero = jnp.zeros_like(q_t)
    q2_t = jnp.concatenate([jnp.where(dim < HEAD_DIM, q_t, zero), jnp.where(dim >= HEAD_DIM, q_t, zero)],
                           axis=1).astype(BF16)
    sub = SB_SUB
    n_sub = tq // sub
    r = lax.broadcasted_iota(jnp.int32, (sub + 8, sub), 0)
    c_ = lax.broadcasted_iota(jnp.int32, (sub + 8, sub), 1)
    sfx = jnp.where((c_ > r) | (r >= sub), 1.0, 0.0).astype(BF16)

    def weigh(j, slot, diag):
        start = pl.multiple_of(j * tq, tq)
        z = jnp.dot(k_ref[0, pl.ds(start, tq), :], q2_t, preferred_element_type=F32)
        neg_abs = lax.bitcast_convert_type(lax.bitcast_convert_type(z, jnp.int32) | jnp.int32(-2 ** 31), F32)
        log_beta = jnp.minimum(z, 0.0) - jnp.log2(1.0 + jnp.exp2(neg_abs))
        l1m = log_beta - z
        if diag:
            before = lax.broadcasted_iota(jnp.int32, (tq, 2 * tq), 0) < (
                lax.broadcasted_iota(jnp.int32, (tq, 2 * tq), 1) & (tq - 1))
            l1m = jnp.where(before, l1m, 0.0)
            log_beta = jnp.where(before, log_beta, NEG_INF)
        lb = l1m.astype(BF16)
        sums = []
        for s in range(n_sub):
            rows = slice(s * sub, (s + 1) * sub)
            ext = jnp.dot(sfx, lb[rows], preferred_element_type=F32)
            g_scr[slot, rows, :] = log_beta[rows] + ext[:sub]
            sums.append(ext[sub:sub + 1])
        return tuple(sums)

    def gather(j, slot, sums, c, acc):
        start = pl.multiple_of(j * tq, tq)
        parts = [None] * n_sub
        for s in reversed(range(n_sub)):
            parts[s] = jnp.exp2(g_scr[slot, s * sub:(s + 1) * sub, :] + c).astype(BF16)
            c = c + sums[s]
        a = jnp.concatenate(parts, axis=0)
        v_t = v_ref[0, pl.ds(start, tq), :].astype(F32).T.astype(BF16)
        acc = acc + jnp.dot(v_t, a, preferred_element_type=F32)
        return c, acc

    odd = i % 2
    sums = weigh(i, odd, True)
    carry = (sums, jnp.zeros((1, 2 * tq), F32), jnp.zeros((LANES, 2 * tq), F32))

    def single(n, cr):
        c1, a1 = gather(i, 1, cr[0], cr[1], cr[2])
        return weigh(i - 1, 0, False), c1, a1

    def alive(c):
        return (jnp.max(c) > SB_DEAD).astype(jnp.int32)

    def carry_after(sums, c):
        for s in reversed(range(n_sub)):
            c = c + sums[s]
        return c

    def double(cr):
        n, _, sums, c, acc = cr
        j = i - odd - 2 * n

        def with_next(_):
            c1, a1 = gather(j, 0, sums, c, acc)
            s1 = weigh(j - 1, 1, False)

            def and_after(_):
                c2, a2 = gather(j - 1, 1, s1, c1, a1)
                return jnp.int32(1), weigh(j - 2, 0, False), c2, a2

            def then_stop(_):
                c2, a2 = gather(j - 1, 1, s1, c1, a1)
                return jnp.int32(0), s1, c2, a2

            return lax.cond(alive(carry_after(s1, c1)) > 0, and_after, then_stop, None)

        def only_this(_):
            c1, a1 = gather(j, 0, sums, c, acc)
            return jnp.int32(0), sums, c1, a1

        live, s2, c2, a2 = lax.cond(alive(carry_after(sums, c)) > 0, with_next, only_this, None)
        return n + 1, live, s2, c2, a2

    n_trips = (i - odd) // 2
    sums, c, acc = lax.fori_loop(0, odd, single, carry)
    n, live, sums, c, acc = lax.while_loop(lambda cr: (cr[0] < n_trips) & (cr[1] > 0), double,
                                           (jnp.int32(0), alive(c), sums, c, acc))
    acc = lax.cond(live > 0, lambda _: gather(i - odd - 2 * n, 0, sums, c, acc)[1], lambda _: acc, None)
    o_ref[0] = jnp.where(dim < HEAD_DIM, acc[:, :tq], acc[:, tq:]).T


def _sb_attention(q, k, v, tq=ATTN_TILE):
    B, S, _ = q.shape
    grid = (B, PAIRS, S // tq)
    return pl.pallas_call(
        functools.partial(_sb_kernel_t, tq=tq), grid=grid,
        in_specs=[pl.BlockSpec((1, tq, LANES), lambda b, p, i: (b, i, p)),
                  pl.BlockSpec((1, S, LANES), lambda b, p, i: (b, 0, p)),
                  pl.BlockSpec((1, S, LANES), lambda b, p, i: (b, 0, p))],
        out_specs=pl.BlockSpec((1, tq, LANES), lambda b, p, i: (b, i, p)),
        out_shape=jax.ShapeDtypeStruct((B, S, D_INNER), F32),
        scratch_shapes=[pltpu.VMEM((2, tq, 2 * tq), F32)],
        compiler_params=_cparams(3), name="sb_attn")(q, k, v)


def _moba_kernel_t(q_ref, k_ref, v_ref, km_ref, o_ref, s_scr, *, nkb, tq):
    i = pl.program_id(2)
    km = km_ref[0]
    per_tile = tq // MOBA_BLOCK
    dim = lax.broadcasted_iota(jnp.int32, (LANES, tq), 0)
    q_t = q_ref[0].astype(F32).T
    zero = jnp.zeros_like(q_t)
    q2_t = jnp.concatenate([jnp.where(dim < HEAD_DIM, q_t, zero), jnp.where(dim >= HEAD_DIM, q_t, zero)],
                           axis=1).astype(BF16)

    km_hi = km.astype(BF16)
    km_lo = (km - km_hi.astype(F32)).astype(BF16)
    gate = jnp.dot(jnp.concatenate([km_hi, km_lo], axis=1), jnp.concatenate([q2_t, q2_t], axis=0),
                   preferred_element_type=F32)
    blk = lax.broadcasted_iota(jnp.int32, (nkb, 2 * tq), 0)
    blk_f = blk.astype(F32)
    qcol = lax.broadcasted_iota(jnp.int32, (nkb, 2 * tq), 1) & (tq - 1)
    q_blk = i * per_tile + qcol // MOBA_BLOCK
    past = blk < q_blk
    g = jnp.where(past, gate, NEG_INF)
    sel = jnp.zeros((nkb, 2 * tq), jnp.bool_)
    for _ in range(MOBA_TOPK):
        mx = jnp.max(g, axis=0, keepdims=True)
        first = jnp.min(jnp.where(g == mx, blk_f, float(nkb)), axis=0, keepdims=True)
        pick = blk_f == first
        sel = sel | (pick & past)
        g = jnp.where(pick, NEG_INF, g)
    bias = jnp.where(sel | (blk == q_blk), 0.0, MASKED).astype(BF16)
    qx_t = jnp.concatenate([q2_t, bias, jnp.zeros((LANES - nkb, 2 * tq), BF16)], axis=0)
    lane_k = lax.broadcasted_iota(jnp.int32, (tq, LANES), 1)
    key_blk = lax.broadcasted_iota(jnp.int32, (tq, LANES), 0) // MOBA_BLOCK

    def score(j, slot, diag):
        st = pl.multiple_of(j * tq, tq)
        onehot = jnp.where(lane_k == j * per_tile + key_blk, 1.0, 0.0).astype(BF16)
        kx = jnp.concatenate([k_ref[0, pl.ds(st, tq), :], onehot], axis=1)
        s = jnp.dot(kx, qx_t, preferred_element_type=F32)
        if diag:
            causal = lax.broadcasted_iota(jnp.int32, (tq, 2 * tq), 0) <= (
                lax.broadcasted_iota(jnp.int32, (tq, 2 * tq), 1) & (tq - 1))
            s = jnp.where(causal, s, NEG_INF)
        s_scr[slot] = s
        return jnp.max(s, axis=0, keepdims=True)

    def absorb(j, slot, m_tile, carry):
        m, l, acc = carry
        st = pl.multiple_of(j * tq, tq)
        v_t = v_ref[0, pl.ds(st, tq), :].astype(F32).T.astype(BF16)
        m_new = jnp.maximum(m, m_tile)
        alpha = jnp.exp2(m - m_new)
        pb = jnp.exp2(s_scr[slot] - m_new).astype(BF16)
        ones = jnp.ones((16, tq), BF16)
        res = [jnp.dot(jnp.concatenate([v_t[h * HEAD_DIM:(h + 1) * HEAD_DIM], ones], axis=0),
                       pb[:, h * tq:(h + 1) * tq], preferred_element_type=F32) for h in range(2)]
        l = alpha * l + jnp.concatenate([r_[HEAD_DIM:HEAD_DIM + 1] for r_ in res], axis=1)
        acc = alpha * acc + jnp.concatenate([r_[:HEAD_DIM] for r_ in res], axis=1)
        return m_new, l, acc

    odd = i % 2
    m_tile = score(i, odd, True)
    state = (jnp.full((1, 2 * tq), NEG_INF, F32), jnp.zeros((1, 2 * tq), F32), jnp.zeros((HEAD_DIM, 2 * tq), F32))

    def single(n, cr):
        st1 = absorb(i, 1, cr[0], cr[1])
        return score(i - 1, 0, False), st1

    def double(n, cr):
        j = i - odd - 2 * n
        st1 = absorb(j, 0, cr[0], cr[1])
        m1 = score(j - 1, 1, False)
        st2 = absorb(j - 1, 1, m1, st1)
        return score(j - 2, 0, False), st2

    carry = lax.fori_loop(0, odd, single, (m_tile, state))
    m_tile, state = lax.fori_loop(0, (i - odd) // 2, double, carry)
    _, l, acc = absorb(0, 0, m_tile, state)
    out = acc / l
    o_ref[0] = jnp.concatenate([out[:, :tq], out[:, tq:]], axis=0).T


def _moba_attention(q, k, v, kmean, tq=ATTN_TILE):
    B, S, _ = q.shape
    nkb = S // MOBA_BLOCK
    grid = (B, PAIRS, S // tq)
    return pl.pallas_call(
        functools.partial(_moba_kernel_t, nkb=nkb, tq=tq), grid=grid,
        in_specs=[pl.BlockSpec((1, tq, LANES), lambda b, p, i: (b, i, p)),
                  pl.BlockSpec((1, S, LANES), lambda b, p, i: (b, 0, p)),
                  pl.BlockSpec((1, S, LANES), lambda b, p, i: (b, 0, p)),
                  pl.BlockSpec((1, nkb, LANES), lambda b, p, i: (b, 0, p))],
        out_specs=pl.BlockSpec((1, tq, LANES), lambda b, p, i: (b, i, p)),
        out_shape=jax.ShapeDtypeStruct((B, S, D_INNER), F32),
        scratch_shapes=[pltpu.VMEM((2, tq, 2 * tq), F32)],
        compiler_params=_cparams(3), name="moba_attn")(q, k, v, kmean)


def _band_kernel(*refs, max_back, kv_heads, use_sinks, want_lse):
    tq = BAND_BLOCK
    pos = 0
    if use_sinks:
        sink_ref = refs[0]
        pos = 1
    q_ref, kp_ref, kc_ref, vp_ref, vc_ref = refs[pos:pos + 5]
    o_ref = refs[pos + 5]
    lse_ref = refs[pos + 6] if want_lse else None
    i = pl.program_id(2)
    rep = N_HEADS // kv_heads

    lane = lax.broadcasted_iota(jnp.int32, (tq, LANES), 1)
    row = lax.broadcasted_iota(jnp.int32, (tq, 2 * tq), 0)
    col = lax.broadcasted_iota(jnp.int32, (tq, 2 * tq), 1)
    dist = row - col + tq
    first_key = jnp.where(i > 0, 0, tq)
    valid = (dist >= 0) & (dist <= max_back) & (col >= first_key)
    low_half = lane < HEAD_DIM

    kcat = jnp.concatenate([kp_ref[0], kc_ref[0]], axis=0)
    vcat = jnp.concatenate([vp_ref[0], vc_ref[0]], axis=0)
    lse_acc = jnp.zeros((tq, LANES), F32)
    for pr in range(PAIRS):
        qf = q_ref[0, :, pr * LANES:(pr + 1) * LANES].astype(F32)
        q_same = qf.astype(BF16)
        q_swap = pltpu.roll(qf, HEAD_DIM, 1).astype(BF16)
        outs = []
        for hh in range(2):
            h = 2 * pr + hh
            g = h // rep
            gh = g % 2
            qsrc = q_same if gh == hh else q_swap
            qm = jnp.where(low_half == (gh == 0), qsrc, jnp.zeros_like(qsrc))
            kt = kcat[:, (g // 2) * LANES:(g // 2 + 1) * LANES]
            vt = vcat[:, (g // 2) * LANES:(g // 2 + 1) * LANES]
            s = jnp.where(valid, _nt_dot(qm, kt), NEG_INF)
            m = jnp.max(s, axis=1, keepdims=True)
            if use_sinks:
                sink = sink_ref[h]
                m = jnp.maximum(m, sink)
            e = jnp.exp(s - m)
            den = jnp.sum(e, axis=1, keepdims=True)
            if use_sinks:
                den = den + jnp.exp(sink - m)
            o = jnp.dot(e.astype(BF16), vt, preferred_element_type=F32) / den
            if gh != hh:
                o = pltpu.roll(o, HEAD_DIM, 1)
            outs.append(o)
            if want_lse:
                lse_acc = jnp.where(lane == h, m + jnp.log(den), lse_acc)
        o_ref[0, :, pr * LANES:(pr + 1) * LANES] = jnp.where(low_half, outs[0], outs[1])
    if want_lse:
        lse_ref[0] = lse_acc


def _band_attention(q, k, v, *, dilation, max_back, kv_heads, sinks=None, want_lse=False):
    B, L, _ = q.shape
    kvw = kv_heads * HEAD_DIM
    tq = BAND_BLOCK
    qv, kv_, vv = q, k, v
    grid = (B, dilation, L // tq)
    cur = lambda b, r, i: (b, i, r)
    prev = lambda b, r, i: (b, jnp.maximum(i - 1, 0), r)
    in_specs = [pl.BlockSpec((1, tq, D_INNER), cur),
                pl.BlockSpec((1, tq, kvw), prev), pl.BlockSpec((1, tq, kvw), cur),
                pl.BlockSpec((1, tq, kvw), prev), pl.BlockSpec((1, tq, kvw), cur)]
    args = [qv, kv_, kv_, vv, vv]
    if sinks is not None:
        in_specs = [pl.BlockSpec(memory_space=pltpu.SMEM)] + in_specs
        args = [sinks] + args
    out_shape = [jax.ShapeDtypeStruct((B, L, dilation * D_INNER), F32)]
    out_specs = [pl.BlockSpec((1, tq, D_INNER), cur)]
    if want_lse:
        out_shape.append(jax.ShapeDtypeStruct((B, L, dilation * LANES), F32))
        out_specs.append(pl.BlockSpec((1, tq, LANES), cur))
    kern = functools.partial(_band_kernel, max_back=max_back, kv_heads=kv_heads,
                             use_sinks=sinks is not None, want_lse=want_lse)
    res = pl.pallas_call(kern, grid=grid, in_specs=in_specs, out_specs=out_specs, out_shape=out_shape,
                         compiler_params=_cparams(3), name="band_attn")(*args)
    return tuple(res) if want_lse else res[0]


def _out_kernel(*refs, n_groups, dils):
    ys = refs[:n_groups]
    pos = n_groups
    if n_groups > 1:
        lses = refs[pos:pos + n_groups]
        expand_ref = refs[pos + n_groups]
        pos += n_groups + 1
    z_ref, x_ref, w_ref, g_ref, b_ref, o_ref = refs[pos:pos + 6]
    y_scr, l_scr = refs[pos + 6:pos + 8] if any(d > 1 for d in dils) else (None, None)

    def token_order(ref, scr, d, width):
        if d == 1:
            return ref[...]
        rows = ref.shape[0]
        tiles = width // LANES
        for r in range(d):
            for t in range(tiles):
                scr[t, pl.ds(r, rows, stride=d), :] = ref[:, r * width + t * LANES:r * width + (t + 1) * LANES]
        return jnp.concatenate([scr[t] for t in range(tiles)], axis=1) if tiles > 1 else scr[0]

    if n_groups == 1:
        y = ys[0][...]
    else:
        ls = [token_order(r, l_scr, d, LANES) for r, d in zip(lses, dils)]
        mx = functools.reduce(jnp.maximum, ls)
        es = [jnp.exp(l - mx) for l in ls]
        tot = functools.reduce(lambda a, b: a + b, es)
        y = None
        for e, yr, d in zip(es, ys, dils):
            w = e / tot
            w_hi = w.astype(BF16)
            w_lo = (w - w_hi.astype(F32)).astype(BF16)
            wts = jnp.dot(jnp.concatenate([w_hi, w_lo], axis=1), expand_ref[...], preferred_element_type=F32)
            yg = wts * token_order(yr, y_scr, d, D_INNER)
            y = yg if y is None else y + yg
    z = z_ref[...]
    u = (y * (z * jax.nn.sigmoid(z))).astype(BF16)
    r = DN_ALPHA * x_ref[...] + jnp.dot(u, w_ref[...], preferred_element_type=F32)
    mu = jnp.mean(r, axis=1, keepdims=True)
    d = r - mu
    var = jnp.mean(d * d, axis=1, keepdims=True)
    o_ref[...] = d * lax.rsqrt(var + LN_EPS) * g_ref[...] + b_ref[...]


def _out_block(ys, lses, z, x2, w_out_bf, g, b, dils=None, tm=256):
    T, D = x2.shape
    n_groups = len(ys)
    dils = dils or (1,) * n_groups
    row = pl.BlockSpec((tm, D), lambda i: (i, 0))
    in_specs = [pl.BlockSpec((tm // d, d * D_INNER), lambda i: (i, 0)) for d in dils]
    args = list(ys)
    if n_groups > 1:
        in_specs += [pl.BlockSpec((tm // d, d * LANES), lambda i: (i, 0)) for d in dils]
        args += list(lses)
        expand = (np.arange(2 * LANES)[:, None] % LANES == (np.arange(D_INNER) // HEAD_DIM)[None, :])
        in_specs.append(pl.BlockSpec((2 * LANES, D_INNER), lambda i: (0, 0)))
        args.append(jnp.asarray(expand, dtype=BF16))
    in_specs += [row, row, pl.BlockSpec((D_INNER, D), lambda i: (0, 0)),
                 pl.BlockSpec((1, D), lambda i: (0, 0)), pl.BlockSpec((1, D), lambda i: (0, 0))]
    args += [z, x2, w_out_bf, g.reshape(1, D), b.reshape(1, D)]
    scratch = ([pltpu.VMEM((D_INNER // LANES, tm, LANES), F32), pltpu.VMEM((1, tm, LANES), F32)]
               if any(d > 1 for d in dils) else [])
    return pl.pallas_call(
        functools.partial(_out_kernel, n_groups=n_groups, dils=dils), grid=(T // tm,),
        in_specs=in_specs, out_specs=row, out_shape=jax.ShapeDtypeStruct((T, D), F32),
        scratch_shapes=scratch, compiler_params=_cparams(1), name="out_ln")(*args)


def kernel(x, sb_w_in, sb_w_out, ln0_g, ln0_b, moba_w_in, moba_w_out, ln1_g, ln1_b,
           swa_w_in, swa_sinks, swa_w_out, ln2_g, ln2_b, dil_w_in, dil_w_out, ln3_g, ln3_b):
    B, S, D = x.shape
    T = B * S
    x2 = x.reshape(T, D)
    rope = _rope_lane_tables(S)
    W = D_INNER

    segs = ((0, W, False, Q_SCALE_LOG2, 0), (W, W, False, 1.0, 1), (2 * W, W, False, 1.0, 2),
            (3 * W, W, False, 1.0, 3))
    q, k, v, z = _project(x2, sb_w_in.astype(BF16), segs, (W, W, W, W), (BF16, BF16, BF16, F32), tm=PROJ_ROWS_WIDE)
    y = _sb_attention(q.reshape(B, S, W), k.reshape(B, S, W), v.reshape(B, S, W))
    x2 = _out_block([y.reshape(T, W)], None, z, x2, sb_w_out.astype(BF16), ln0_g, ln0_b)

    segs = ((0, W, True, Q_SCALE_LOG2, 0), (W, W, True, 1.0, 1), (2 * W, W, False, 1.0, 2),
            (3 * W, W, False, 1.0, 3))
    q, k, v, z, kmean = _project(x2, moba_w_in.astype(BF16), segs, (W, W, W, W), (BF16, BF16, BF16, F32),
                                 rope_tabs=rope, kmean_seg=1, tm=PROJ_ROWS_WIDE)
    y = _moba_attention(q.reshape(B, S, W), k.reshape(B, S, W), v.reshape(B, S, W),
                        kmean.reshape(B, S // MOBA_BLOCK, W))
    x2 = _out_block([y.reshape(T, W)], None, z, x2, moba_w_out.astype(BF16), ln1_g, ln1_b)

    kvw = SWA_KV_HEADS * HEAD_DIM
    segs = ((0, W, True, Q_SCALE, 0), (W, kvw, True, 1.0, 1), (W + kvw, kvw, False, 1.0, 2),
            (W + 2 * kvw, W, False, 1.0, 3))
    q, k, v, z = _project(x2, swa_w_in.astype(BF16), segs, (W, kvw, kvw, W), (BF16, BF16, BF16, F32),
                          rope_tabs=rope, tm=PROJ_ROWS_WIDE)
    y = _band_attention(q.reshape(B, S, W), k.reshape(B, S, kvw), v.reshape(B, S, kvw), dilation=1,
                        max_back=SWA_WINDOW - 1, kv_heads=SWA_KV_HEADS, sinks=swa_sinks.astype(F32))
    x2 = _out_block([y.reshape(T, W)], None, z, x2, swa_w_out.astype(BF16), ln2_g, ln2_b)

    n_g = len(DILATED_GROUPS)
    segs = []
    for g in range(n_g):
        segs += [((3 * g) * W, W, True, Q_SCALE, 3 * g), ((3 * g + 1) * W, W, True, 1.0, 3 * g + 1),
                 ((3 * g + 2) * W, W, False, 1.0, 3 * g + 2)]
    segs.append((3 * n_g * W, W, False, 1.0, 3 * n_g))
    dils = tuple(d for _, d in DILATED_GROUPS)
    out_dils = tuple(d for d in dils for _ in range(3)) + (1,)
    outs = _project(x2, dil_w_in.astype(BF16), tuple(segs), (W,) * (3 * n_g + 1), (BF16,) * (3 * n_g) + (F32,),
                    rope_tabs=rope, out_dils=out_dils)
    z = outs[-1]
    ys, lses = [], []
    for g, (window, dil) in enumerate(DILATED_GROUPS):
        qg, kg, vg = (outs[3 * g + t].reshape(B, S // dil, dil * W) for t in range(3))
        o, lse = _band_attention(qg, kg, vg, dilation=dil, max_back=window // dil, kv_heads=N_HEADS,
                                 want_lse=True)
        ys.append(o.reshape(T // dil, dil * W))
        lses.append(lse.reshape(T // dil, dil * LANES))
    x2 = _out_block(ys, lses, z, x2, dil_w_out.astype(BF16), ln3_g, ln3_b, dils=dils)
    return x2.reshape(B, S, D)
```
